```python
import math
import jax, jax.numpy as jnp
from jax import lax
import numpy as np

D_MODEL = 1024
BATCH = 4
SEQ = 8192
DEPTH = 1
DEC_BATCH = 128
DEC_SEQ = 4
PAST_LEN = 8192
PAGE_SIZE = 128

MIX_WIDTH = D_MODEL
SB_WIDTH = MIX_WIDTH // 2
SB_HEADS = 8
SB_HEAD_DIM = SB_WIDTH // SB_HEADS
SB_BIAS_INIT = -6.0
SSM_WIDTH = MIX_WIDTH - SB_WIDTH
SSM_GROUP = 16
SSM_GROUPS = SSM_WIDTH // SSM_GROUP
SSM_STATE = 64
Q_BLOCK = 128
IN_WIDTH = 4 * SB_WIDTH + 2 * SSM_WIDTH
NORM_EPS = 1e-6
DT_MIN = 1e-3
DT_MAX = 1e-1

kernel_name = 'hymba_stickbreak_s5_decoder_step'


def rmsnorm(x, g):
    xf = x.astype(jnp.float32)
    y = xf * lax.rsqrt(jnp.mean(xf * xf, axis=-1, keepdims=True) + NORM_EPS)
    return (y * g.astype(jnp.float32)).astype(x.dtype)


def mixer_inputs(x, c, norm_g, w_ada, b_ada, w_in):
    m = (jax.nn.silu(c) @ w_ada + b_ada)[:, None, :]
    shift, scale, gate = jnp.split(m, 3, axis=-1)
    h = rmsnorm(x, norm_g) * (1 + scale) + shift
    proj = h @ w_in
    q, k, v, g_sb, u, g_ssm = jnp.split(
        proj, [SB_WIDTH, 2 * SB_WIDTH, 3 * SB_WIDTH, 4 * SB_WIDTH, 4 * SB_WIDTH + SSM_WIDTH], axis=-1)
    n, t, _ = x.shape
    hd = (n, t, SB_HEADS, SB_HEAD_DIM)
    return q.reshape(hd), k.reshape(hd), v.reshape(hd), g_sb, u, g_ssm, gate


def stick_breaking(q, k, v, bias, q_pos, k_pos):
    f32 = jnp.float32
    z = (jnp.einsum('...qhd,...khd->...hqk', q.astype(f32), k.astype(f32)) * (SB_HEAD_DIM ** -0.5)
         + bias.astype(f32)[:, None, None])
    mask = k_pos[None, :] < q_pos[:, None]
    log_beta = jax.nn.log_sigmoid(z)
    log_keep = jnp.where(mask, jax.nn.log_sigmoid(-z), 0.0)
    later = lax.cumsum(log_keep, axis=z.ndim - 1, reverse=True) - log_keep
    a = jnp.where(mask, jnp.exp(log_beta + later), 0.0)
    out = jnp.einsum('...hqk,...khd->...qhd', a, v.astype(f32))
    return out.astype(q.dtype)


def prompt_attention(q, k, v, bias):
    t = q.shape[1]
    pos = jnp.arange(t)
    outs = []
    for i in range(t // Q_BLOCK):
        lo, hi = i * Q_BLOCK, (i + 1) * Q_BLOCK
        outs.append(stick_breaking(q[:, lo:hi], k[:, :hi], v[:, :hi], bias, pos[lo:hi], pos[:hi]))
    return jnp.concatenate(outs, axis=1)


def sample_attention(q, k, v, bias, cache_k, cache_v, page_table, layer):
    ds = q.shape[1]
    past = page_table.shape[1] * PAGE_SIZE
    q_pos = past + jnp.arange(ds)
    k_pos = jnp.arange(past + ds)

    def one(args):
        qb, kb, vb, pages = args
        k_past = cache_k[layer, pages].reshape(past, SB_HEADS, SB_HEAD_DIM).astype(kb.dtype)
        v_past = cache_v[layer, pages].reshape(past, SB_HEADS, SB_HEAD_DIM).astype(vb.dtype)
        k_all = jnp.concatenate([k_past, kb], axis=0)
        v_all = jnp.concatenate([v_past, vb], axis=0)
        return stick_breaking(qb, k_all, v_all, bias, q_pos, k_pos)

    return lax.map(one, (q, k, v, page_table))


def s5_discretize(a_re, a_im, log_dt, b_re, b_im):
    f32 = jnp.float32
    dt = jnp.exp(log_dt.astype(f32))[:, None]
    are, aim = a_re.astype(f32), a_im.astype(f32)
    mag = jnp.exp(dt * are)
    abar_re = mag * jnp.cos(dt * aim)
    abar_im = mag * jnp.sin(dt * aim)
    den = are * are + aim * aim
    n_re = abar_re - 1.0
    f_re = (n_re * are + abar_im * aim) / den
    f_im = (abar_im * are - n_re * aim) / den
    br, bi = b_re.astype(f32), b_im.astype(f32)
    bbar_re = f_re[..., None] * br - f_im[..., None] * bi
    bbar_im = f_re[..., None] * bi + f_im[..., None] * br
    return abar_re, abar_im, bbar_re, bbar_im


def complex_affine_combine(e1, e2):
    a1r, a1i, b1r, b1i = e1
    a2r, a2i, b2r, b2i = e2
    return (a1r * a2r - a1i * a2i,
            a1r * a2i + a1i * a2r,
            a2r * b1r - a2i * b1i + b2r,
            a2r * b1i + a2i * b1r + b2i)


def s5_branch(u, h0_re, h0_im, a_re, a_im, log_dt, b_re, b_im, c_re, c_im, d_skip, w_glu, b_glu):
    f32 = jnp.float32
    n, t, _ = u.shape
    uf = u.astype(f32).reshape(n, t, SSM_GROUPS, SSM_GROUP)
    abar_re, abar_im, bbar_re, bbar_im = s5_discretize(a_re, a_im, log_dt, b_re, b_im)
    bu_re = jnp.einsum('ntgc,gpc->ntgp', uf, bbar_re)
    bu_im = jnp.einsum('ntgc,gpc->ntgp', uf, bbar_im)
    h0r, h0i = h0_re.astype(f32), h0_im.astype(f32)
    bu_re = bu_re.at[:, 0].add(abar_re * h0r - abar_im * h0i)
    bu_im = bu_im.at[:, 0].add(abar_re * h0i + abar_im * h0r)
    a_r = jnp.broadcast_to(abar_re, (1, t) + abar_re.shape)
    a_i = jnp.broadcast_to(abar_im, (1, t) + abar_im.shape)
    _, _, h_re, h_im = lax.associative_scan(complex_affine_combine, (a_r, a_i, bu_re, bu_im), axis=1)
    y = (jnp.einsum('ntgp,gcp->ntgc', h_re, c_re.astype(f32))
         - jnp.einsum('ntgp,gcp->ntgc', h_im, c_im.astype(f32))
         + d_skip.astype(f32) * uf)
    y = jax.nn.gelu(y).reshape(n, t, SSM_WIDTH)
    z = y @ w_glu.astype(f32) + b_glu.astype(f32)
    val, gt = jnp.split(z, 2, axis=-1)
    out = (val * jax.nn.sigmoid(gt)).astype(u.dtype)
    return out, h_re[:, -1], h_im[:, -1]


def mixer_output(x, gate, att, g_sb, ssm, g_ssm, w_out):
    n, t, _ = x.shape
    mixed = jnp.concatenate([att.reshape(n, t, SB_WIDTH) * jax.nn.silu(g_sb),
                             ssm * jax.nn.silu(g_ssm)], axis=-1)
    return x + gate * (mixed @ w_out)


def setup_inputs(seed: int = 0) -> dict:
    key = jax.random.key(seed)
    ks = jax.random.split(key, 32)
    f32 = jnp.float32
    n_pages = PAST_LEN // PAGE_SIZE
    n_used = DEC_BATCH * n_pages
    n_pool = n_used + max(1, n_used // 4)

    def nrm(k, shape, s=1.0):
        return jax.random.normal(k, shape, f32) * s

    page_table = jax.random.permutation(ks[6], n_pool)[:n_used].reshape(DEC_BATCH, n_pages).astype(jnp.int32)
    n_idx = jnp.arange(SSM_STATE, dtype=f32)
    return {
        'x_prompt': nrm(ks[0], (BATCH, SEQ, D_MODEL)),
        'x_sample': nrm(ks[1], (DEC_BATCH, DEC_SEQ, D_MODEL)),
        'c_prompt': nrm(ks[2], (BATCH, D_MODEL)),
        'c_sample': nrm(ks[3], (DEC_BATCH, D_MODEL)),
        'cache_k': nrm(ks[4], (DEPTH, n_pool, PAGE_SIZE, SB_HEADS, SB_HEAD_DIM)),
        'cache_v': nrm(ks[5], (DEPTH, n_pool, PAGE_SIZE, SB_HEADS, SB_HEAD_DIM)),
        'page_table': page_table,
        'state_ssm_re': nrm(ks[7], (DEPTH, DEC_BATCH, SSM_GROUPS, SSM_STATE), 0.3),
        'state_ssm_im': nrm(ks[8], (DEPTH, DEC_BATCH, SSM_GROUPS, SSM_STATE), 0.3),
        'norm_g': 1.0 + nrm(ks[9], (DEPTH, D_MODEL), 0.01),
        'w_ada': nrm(ks[10], (DEPTH, D_MODEL, 3 * D_MODEL), D_MODEL ** -0.5),
        'b_ada': nrm(ks[11], (DEPTH, 3 * D_MODEL), 0.01),
        'w_in': nrm(ks[12], (DEPTH, D_MODEL, IN_WIDTH), D_MODEL ** -0.5),
        'sb_bias': SB_BIAS_INIT + nrm(ks[25], (DEPTH, SB_HEADS), 0.1),
        'ssm_a_re': -0.5 + nrm(ks[13], (DEPTH, SSM_GROUPS, SSM_STATE), 0.01),
        'ssm_a_im': math.pi * n_idx + nrm(ks[14], (DEPTH, SSM_GROUPS, SSM_STATE), 0.01),
        'ssm_log_dt': jax.random.uniform(ks[15], (DEPTH, SSM_GROUPS), f32, math.log(DT_MIN), math.log(DT_MAX)),
        'ssm_b_re': nrm(ks[16], (DEPTH, SSM_GROUPS, SSM_STATE, SSM_GROUP), (2 * SSM_GROUP) ** -0.5),
        'ssm_b_im': nrm(ks[17], (DEPTH, SSM_GROUPS, SSM_STATE, SSM_GROUP), (2 * SSM_GROUP) ** -0.5),
        'ssm_c_re': nrm(ks[18], (DEPTH, SSM_GROUPS, SSM_GROUP, SSM_STATE), SSM_STATE ** -0.5),
        'ssm_c_im': nrm(ks[19], (DEPTH, SSM_GROUPS, SSM_GROUP, SSM_STATE), SSM_STATE ** -0.5),
        'ssm_d': nrm(ks[20], (DEPTH, SSM_GROUPS, SSM_GROUP)),
        'w_glu': nrm(ks[21], (DEPTH, SSM_WIDTH, 2 * SSM_WIDTH), SSM_WIDTH ** -0.5),
        'b_glu': nrm(ks[22], (DEPTH, 2 * SSM_WIDTH), 0.01),
        'w_out': nrm(ks[23], (DEPTH, MIX_WIDTH, D_MODEL), MIX_WIDTH ** -0.5),
        'final_norm_g': 1.0 + nrm(ks[24], (D_MODEL,), 0.01),
    }


def reference(x_prompt, x_sample, c_prompt, c_sample, cache_k, cache_v, page_table,
              state_ssm_re, state_ssm_im, norm_g, w_ada, b_ada, w_in, sb_bias, ssm_a_re, ssm_a_im,
              ssm_log_dt, ssm_b_re, ssm_b_im, ssm_c_re, ssm_c_im, ssm_d, w_glu, b_glu,
              w_out, final_norm_g):
    xp, xs = x_prompt, x_sample
    kp_l, vp_l, hrp_l, hip_l = [], [], [], []
    ks_l, vs_l, hrs_l, his_l = [], [], [], []
    h0_prompt = jnp.zeros((xp.shape[0], SSM_GROUPS, SSM_STATE), jnp.float32)
    for l in range(DEPTH):
        ssm_p = (ssm_a_re[l], ssm_a_im[l], ssm_log_dt[l], ssm_b_re[l], ssm_b_im[l],
                 ssm_c_re[l], ssm_c_im[l], ssm_d[l], w_glu[l], b_glu[l])
        q, k, v, g_sb, u, g_ssm, gate = mixer_inputs(xp, c_prompt, norm_g[l], w_ada[l], b_ada[l], w_in[l])
        att = prompt_attention(q, k, v, sb_bias[l])
        ssm, hr, hi = s5_branch(u, h0_prompt, h0_prompt, *ssm_p)
        xp = mixer_output(xp, gate, att, g_sb, ssm, g_ssm, w_out[l])
        kp_l.append(k); vp_l.append(v); hrp_l.append(hr); hip_l.append(hi)
        q, k, v, g_sb, u, g_ssm, gate = mixer_inputs(xs, c_sample, norm_g[l], w_ada[l], b_ada[l], w_in[l])
        att = sample_attention(q, k, v, sb_bias[l], cache_k, cache_v, page_table, l)
        ssm, hr, hi = s5_branch(u, state_ssm_re[l], state_ssm_im[l], *ssm_p)
        xs = mixer_output(xs, gate, att, g_sb, ssm, g_ssm, w_out[l])
        ks_l.append(k); vs_l.append(v); hrs_l.append(hr); his_l.append(hi)
    y_prompt = rmsnorm(xp, final_norm_g)
    y_sample = rmsnorm(xs, final_norm_g)
    return (y_prompt, y_sample,
            jnp.stack(kp_l), jnp.stack(vp_l), jnp.stack(hrp_l), jnp.stack(hip_l),
            jnp.stack(ks_l), jnp.stack(vs_l), jnp.stack(hrs_l), jnp.stack(his_l))
```

```python
import functools

import jax
import jax.numpy as jnp
from jax import lax
from jax.experimental import pallas as pl
from jax.experimental.pallas import tpu as pltpu

F32 = jnp.float32
BF16 = jnp.bfloat16
NORM_EPS = 1e-6
SUBLANES = 8
VMEM_LIMIT = 48 * 1024 * 1024

PROMPT_ROWS = 512
ATTN_BLOCK = 256
SSM_ROWS = 256
SCAN_LANES = 512
SAMPLE_PAGES = 8


def _params(*sem):
    return pltpu.CompilerParams(dimension_semantics=sem, vmem_limit_bytes=VMEM_LIMIT)


def _silu(x):
    return x * jax.nn.sigmoid(x)


def _dot(a, b):
    return jnp.dot(a, b, preferred_element_type=F32)


def _dot_nt(a, b):
    return lax.dot_general(a, b, (((1,), (1,)), ((), ())), preferred_element_type=F32)


def _ada_kernel(c_ref, w_ref, b_ref, o_ref):
    c = c_ref[...]
    o_ref[...] = _dot(_silu(c).astype(BF16), w_ref[...].astype(BF16)) + b_ref[...]


def _ada(c, w, b):
    n, d = c.shape
    n3 = w.shape[1]
    return pl.pallas_call(
        _ada_kernel,
        grid=(n3 // d,),
        in_specs=[pl.BlockSpec((n, d), lambda j: (0, 0)),
                  pl.BlockSpec((d, d), lambda j: (0, j)),
                  pl.BlockSpec((1, d), lambda j: (0, j))],
        out_specs=pl.BlockSpec((n, d), lambda j: (0, j)),
        out_shape=jax.ShapeDtypeStruct((n, n3), F32),
        compiler_params=_params("arbitrary"),
        name="ada",
    )(c, w, b)


def _cmul(a, b):
    return a[0] * b[0] - a[1] * b[1], a[0] * b[1] + a[1] * b[0]


def _ssm_param_kernel(are_ref, aim_ref, ldt_ref, bre_ref, bim_ref, coef_ref, wb_ref):
    gp = are_ref.shape[1]
    are, aim = are_ref[...], aim_ref[...]
    dt = jnp.exp(ldt_ref[...])
    mag = jnp.exp(dt * are)
    ar = mag * jnp.cos(dt * aim)
    ai = mag * jnp.sin(dt * aim)
    den = are * are + aim * aim
    nre = ar - 1.0
    fre = (nre * are + ai * aim) / den
    fim = (ai * are - nre * aim) / den
    bre, bim = bre_ref[...], bim_ref[...]
    wb_ref[:, :gp] = (fre * bre - fim * bim).astype(BF16)
    wb_ref[:, gp:] = (fre * bim + fim * bre).astype(BF16)

    pw = [(ar, ai)]
    for _ in range(SUBLANES - 1):
        pw.append(_cmul(pw[-1], (ar, ai)))
    row = lax.broadcasted_iota(jnp.int32, (SUBLANES, gp), 0)
    for n, d in enumerate((1, 2, 4)):
        for part in range(2):
            coef_ref[2 * n + part] = jnp.where(row >= d, pw[d - 1][part], 0.0)
    for part in range(2):
        acc = jnp.zeros((SUBLANES, gp), F32)
        for r in range(SUBLANES):
            acc = jnp.where(row == r, pw[r][part], acc)
        coef_ref[6 + part] = acc


def _ssm_param(are, aim, ldt, bre_bd, bim_bd):
    gc, gp = bre_bd.shape
    return pl.pallas_call(
        _ssm_param_kernel,
        out_shape=(jax.ShapeDtypeStruct((8, SUBLANES, gp), F32),
                   jax.ShapeDtypeStruct((gc, 2 * gp), BF16)),
        compiler_params=_params(),
        name="ssm_param",
    )(are, aim, ldt, bre_bd, bim_bd)


def _inproj_kernel(x_ref, ng_ref, shift_ref, scale_ref, w_ref,
                   q_ref, k_ref, v_ref, gsb_ref, u_ref, gssm_ref, *, qscale):
    x = x_ref[...]
    ms = jnp.mean(x * x, axis=-1, keepdims=True)
    h = x * lax.rsqrt(ms + NORM_EPS) * ng_ref[...]
    h = (h * (1.0 + scale_ref[...]) + shift_ref[...]).astype(BF16)
    w = q_ref.shape[-1]

    def proj(c):
        return _dot(h, w_ref[:, c * w:(c + 1) * w])

    q_ref[...] = proj(0) * qscale
    k_ref[...] = proj(1)
    v_ref[...] = proj(2)
    gsb_ref[...] = _silu(proj(3))
    u_ref[...] = proj(4)
    gssm_ref[...] = _silu(proj(5))


def _inproj_prompt(x, ng, m3, w_in, sbw, qscale):
    b, t, d = x.shape
    tm = min(PROMPT_ROWS, t)
    row = pl.BlockSpec((None, tm, d), lambda i, j: (i, j, 0))
    out = pl.BlockSpec((None, tm, sbw), lambda i, j: (i, j, 0))
    return pl.pallas_call(
        functools.partial(_inproj_kernel, qscale=qscale),
        grid=(b, t // tm),
        in_specs=[row,
                  pl.BlockSpec((1, d), lambda i, j: (0, 0)),
                  pl.BlockSpec((None, 1, d), lambda i, j: (i, 0, 0)),
                  pl.BlockSpec((None, 1, d), lambda i, j: (i, 0, 1)),
                  pl.BlockSpec(w_in.shape, lambda i, j: (0, 0))],
        out_specs=[out] * 6,
        out_shape=[jax.ShapeDtypeStruct((b, t, sbw), F32)] * 6,
        compiler_params=_params("arbitrary", "arbitrary"),
        name="inproj_prompt",
    )(x, ng, m3, m3, w_in)


def _inproj_sample(x2, ng, m, w_in, sbw, qscale, ts):
    nb = x2.shape[0]
    d = x2.shape[1] // ts
    out = pl.BlockSpec((nb, sbw), lambda t: (0, t))
    return pl.pallas_call(
        functools.partial(_inproj_kernel, qscale=qscale),
        grid=(ts,),
        in_specs=[pl.BlockSpec((nb, d), lambda t: (0, t)),
                  pl.BlockSpec((1, d), lambda t: (0, 0)),
                  pl.BlockSpec((nb, d), lambda t: (0, 0)),
                  pl.BlockSpec((nb, d), lambda t: (0, 1)),
                  pl.BlockSpec(w_in.shape, lambda t: (0, 0))],
        out_specs=[out] * 6,
        out_shape=[jax.ShapeDtypeStruct((nb, ts * sbw), F32)] * 6,
        compiler_params=_params("arbitrary"),
        name="inproj_sample",
    )(x2, ng, m, m, w_in)


def _stick_tile(z, later_mat, mask):
    t = jnp.exp(-jnp.abs(z))
    ls = jnp.minimum(z, 0.0) - jnp.log(1.0 + t)
    lk = ls - z
    if mask is not None:
        lk = jnp.where(mask, lk, 0.0)
    hi = lk.astype(BF16)
    lo = (lk - hi.astype(F32)).astype(BF16)
    later = _dot(hi, later_mat) + _dot(lo, later_mat)
    tot = later[:, :1] + lk[:, :1]
    return ls, later, tot


def _later_matrix(n):
    r = lax.broadcasted_iota(jnp.int32, (n, n), 0)
    c = lax.broadcasted_iota(jnp.int32, (n, n), 1)
    return (r > c).astype(BF16)


def _pattn_kernel(bias_ref, q_ref, k_ref, v_ref, o_ref, *, dh):
    tq = q_ref.shape[0]
    hp = pl.program_id(1)
    i = pl.program_id(2)
    first = lax.broadcasted_iota(jnp.int32, (1, 2 * dh), 1) < dh
    q = q_ref[...]
    qh = (jnp.where(first, q, 0.0).astype(BF16), jnp.where(first, 0.0, q).astype(BF16))
    bias = (bias_ref[2 * hp], bias_ref[2 * hp + 1])
    later_mat = _later_matrix(tq)
    r = lax.broadcasted_iota(jnp.int32, (tq, tq), 0)
    c = lax.broadcasted_iota(jnp.int32, (tq, tq), 1)
    causal = c < r

    def tile(j, carry, mask):
        acc, cs = carry
        r0 = pl.multiple_of(j * tq, tq)
        kb = k_ref[pl.ds(r0, tq), :].astype(BF16)
        vb = v_ref[pl.ds(r0, tq), :].astype(BF16)
        outs, new_cs = [], []
        for h in range(2):
            z = _dot_nt(qh[h], kb) + bias[h]
            ls, later, tot = _stick_tile(z, later_mat, mask)
            a = jnp.exp(ls + later + cs[h])
            if mask is not None:
                a = jnp.where(mask, a, 0.0)
            outs.append(_dot(a.astype(BF16), vb))
            new_cs.append(cs[h] + tot)
        return acc + jnp.where(first, outs[0], outs[1]), tuple(new_cs)

    zero_c = jnp.zeros((tq, 1), F32)
    carry = tile(i, (jnp.zeros((tq, 2 * dh), F32), (zero_c, zero_c)), causal)
    carry = lax.fori_loop(0, i, lambda n, cr: tile(i - 1 - n, cr, None), carry)
    o_ref[...] = carry[0]


def _pattn(bias, q, k, v, dh):
    b, t, sbw = q.shape
    tq = min(ATTN_BLOCK, t)
    blk = pl.BlockSpec((None, tq, 2 * dh), lambda bi, hp, i: (bi, i, hp))
    full = pl.BlockSpec((None, t, 2 * dh), lambda bi, hp, i: (bi, 0, hp))
    return pl.pallas_call(
        functools.partial(_pattn_kernel, dh=dh),
        grid=(b, sbw // (2 * dh), t // tq),
        in_specs=[pl.BlockSpec(memory_space=pltpu.SMEM), blk, full, full],
        out_specs=blk,
        out_shape=jax.ShapeDtypeStruct((b, t, sbw), F32),
        compiler_params=_params("arbitrary", "arbitrary", "arbitrary"),
        name="pattn",
    )(bias, q, k, v)


def _ssm_readout(hre, him, u, gs, cre_ref, cim_ref, d_ref, wglu_ref, bglu_ref):
    y = _dot(hre.astype(BF16), cre_ref[...]) - _dot(him.astype(BF16), cim_ref[...]) + d_ref[...] * u
    y = jax.nn.gelu(y)
    z = _dot(y.astype(BF16), wglu_ref[...]) + bglu_ref[...]
    w = z.shape[-1] // 2
    return z[:, :w] * jax.nn.sigmoid(z[:, w:]) * gs


def _pssm_kernel(u_ref, gs_ref, wb_ref, coef_ref, cre_ref, cim_ref, d_ref, wglu_ref, bglu_ref,
                 o_ref, hre_ref, him_ref, st_ref, carry_ref):
    tt = u_ref.shape[0]
    gp = cre_ref.shape[0]
    j = pl.program_id(1)

    @pl.when(j == 0)
    def _():
        carry_ref[...] = jnp.zeros_like(carry_ref)

    u = u_ref[...]
    st_ref[...] = _dot(u.astype(BF16), wb_ref[...])

    for lc in range(gp // SCAN_LANES):
        lre = pl.ds(lc * SCAN_LANES, SCAN_LANES)
        lim = pl.ds(gp + lc * SCAN_LANES, SCAN_LANES)

        def group(g, carry, lre=lre, lim=lim):
            cr, ci = carry
            rows = pl.ds(pl.multiple_of(g * SUBLANES, SUBLANES), SUBLANES)
            xr = st_ref[rows, lre]
            xi = st_ref[rows, lim]
            for n, d in enumerate((1, 2, 4)):
                ar = coef_ref[2 * n, :, lre]
                ai = coef_ref[2 * n + 1, :, lre]
                sr = pltpu.roll(xr, d, 0)
                si = pltpu.roll(xi, d, 0)
                xr, xi = xr + (ar * sr - ai * si), xi + (ar * si + ai * sr)
            pr = coef_ref[6, :, lre]
            pi = coef_ref[7, :, lre]
            hr = xr + (pr * cr - pi * ci)
            hi = xi + (pr * ci + pi * cr)
            st_ref[rows, lre] = hr
            st_ref[rows, lim] = hi
            return hr[SUBLANES - 1:, :], hi[SUBLANES - 1:, :]

        cr, ci = lax.fori_loop(0, tt // SUBLANES, group, (carry_ref[:, lre], carry_ref[:, lim]))
        carry_ref[:, lre] = cr
        carry_ref[:, lim] = ci

    o_ref[...] = _ssm_readout(st_ref[:, :gp], st_ref[:, gp:], u, gs_ref[...],
                              cre_ref, cim_ref, d_ref, wglu_ref, bglu_ref)
    hre_ref[...] = carry_ref[:, :gp]
    him_ref[...] = carry_ref[:, gp:]


def _const(shape):
    nd = len(shape)
    return pl.BlockSpec(shape, lambda *_: (0,) * nd)


def _pssm(u, gs, wb, coef, cre, cim, dsk, wglu, bglu):
    b, t, ssw = u.shape
    gp = cre.shape[0]
    tt = min(SSM_ROWS, t)
    row = pl.BlockSpec((None, tt, ssw), lambda i, j: (i, j, 0))
    st = pl.BlockSpec((None, 1, gp), lambda i, j: (i, 0, 0))
    return pl.pallas_call(
        _pssm_kernel,
        grid=(b, t // tt),
        in_specs=[row, row, _const(wb.shape), _const(coef.shape), _const(cre.shape), _const(cim.shape),
                  _const(dsk.shape), _const(wglu.shape), _const(bglu.shape)],
        out_specs=[row, st, st],
        out_shape=[jax.ShapeDtypeStruct((b, t, ssw), F32),
                   jax.ShapeDtypeStruct((b, 1, gp), F32),
                   jax.ShapeDtypeStruct((b, 1, gp), F32)],
        scratch_shapes=[pltpu.VMEM((tt, 2 * gp), F32), pltpu.VMEM((1, 2 * gp), F32)],
        compiler_params=_params("arbitrary", "arbitrary"),
        name="pssm",
    )(u, gs, wb, coef, cre, cim, dsk, wglu, bglu)


def _sssm_kernel(u_ref, gs_ref, h0re_ref, h0im_ref, wb_ref, coef_ref, cre_ref, cim_ref, d_ref,
                 wglu_ref, bglu_ref, o_ref, hre_ref, him_ref):
    gp = cre_ref.shape[0]
    t = pl.program_id(0)

    @pl.when(t == 0)
    def _():
        hre_ref[...] = h0re_ref[...]
        him_ref[...] = h0im_ref[...]

    u = u_ref[...]
    bu = _dot(u.astype(BF16), wb_ref[...])
    ar = coef_ref[6, 0:1, :]
    ai = coef_ref[7, 0:1, :]
    hr, hi = hre_ref[...], him_ref[...]
    nr = ar * hr - ai * hi + bu[:, :gp]
    ni = ar * hi + ai * hr + bu[:, gp:]
    hre_ref[...] = nr
    him_ref[...] = ni
    o_ref[...] = _ssm_readout(nr, ni, u, gs_ref[...], cre_ref, cim_ref, d_ref, wglu_ref, bglu_ref)


def _sssm(u2, gs2, h0re, h0im, wb, coef, cre, cim, dsk, wglu, bglu, ts):
    nb = u2.shape[0]
    ssw = u2.shape[1] // ts
    gp = cre.shape[0]
    row = pl.BlockSpec((nb, ssw), lambda t: (0, t))
    return pl.pallas_call(
        _sssm_kernel,
        grid=(ts,),
        in_specs=[row, row, _const(h0re.shape), _const(h0im.shape), _const(wb.shape), _const(coef.shape),
                  _const(cre.shape), _const(cim.shape), _const(dsk.shape), _const(wglu.shape),
                  _const(bglu.shape)],
        out_specs=[row, _const((nb, gp)), _const((nb, gp))],
        out_shape=[jax.ShapeDtypeStruct((nb, ts * ssw), F32),
                   jax.ShapeDtypeStruct((nb, gp), F32),
                   jax.ShapeDtypeStruct((nb, gp), F32)],
        compiler_params=_params("arbitrary"),
        name="sssm",
    )(u2, gs2, h0re, h0im, wb, coef, cre, cim, dsk, wglu, bglu)


def _sattn_kernel(pt_ref, q_ref, kn_ref, vn_ref, bias_ref, *refs, npg, heads, dh):
    kp, vp = refs[:npg], refs[npg:2 * npg]
    o_ref, qbd_ref, pad_ref, acc_ref, c_ref = refs[2 * npg:]
    del pt_ref
    ts = q_ref.shape[0]
    page = kp[0].shape[0]
    rows = ts * heads
    c = pl.program_id(1)
    later_mat = _later_matrix(page)
    bias = bias_ref[...]
    head_match = (lax.broadcasted_iota(jnp.int32, (heads, heads * dh), 0)
                  == lax.broadcasted_iota(jnp.int32, (heads, heads * dh), 1) // dh)

    @pl.when(c == 0)
    def _():
        q = q_ref[...]
        for t in range(ts):
            qrow = jnp.broadcast_to(q[t:t + 1, :], (heads, heads * dh))
            qbd_ref[t * heads:(t + 1) * heads, :] = jnp.where(head_match, qrow, 0.0).astype(BF16)
        pad_ref[...] = jnp.zeros_like(pad_ref)
        pad_ref[0:ts, :] = kn_ref[...]
        z = _dot_nt(qbd_ref[...], pad_ref[...].astype(BF16)) + bias[:rows]
        key = lax.broadcasted_iota(jnp.int32, (rows, page), 1)
        qidx = lax.broadcasted_iota(jnp.int32, (rows, page), 0) // heads
        mask = key < qidx
        ls, later, tot = _stick_tile(z, later_mat, mask)
        a = jnp.where(mask, jnp.exp(ls + later), 0.0)
        pad_ref[0:ts, :] = vn_ref[...]
        acc_ref[...] = _dot(a.astype(BF16), pad_ref[...].astype(BF16))
        c_ref[...] = tot

    qbd = qbd_ref[...]
    z = jnp.concatenate([_dot_nt(qbd, kp[j][...].astype(BF16)) for j in range(npg)], axis=0) + bias
    ls, later, tot = _stick_tile(z, later_mat, None)
    run = c_ref[...]
    pieces = []
    for j in range(npg):
        pieces.append(run)
        run = run + tot[j * rows:(j + 1) * rows]
    c_ref[...] = run
    a = jnp.exp(ls + later + jnp.concatenate(pieces, axis=0)).astype(BF16)
    acc = acc_ref[...]
    for j in range(npg):
        acc = acc + _dot(a[j * rows:(j + 1) * rows], vp[j][...].astype(BF16))
    acc_ref[...] = acc

    @pl.when(c == pl.num_programs(1) - 1)
    def _():
        for t in range(ts):
            blk = jnp.where(head_match, acc[t * heads:(t + 1) * heads, :], 0.0)
            o_ref[t:t + 1, :] = jnp.sum(blk, axis=0, keepdims=True)


def _sattn(page_table, q3, kn3, vn3, bias_col, ck, cv, heads, dh):
    nb, ts, sbw = q3.shape
    npages = page_table.shape[1]
    page = ck.shape[1]
    npg = min(SAMPLE_PAGES, npages)
    rows = ts * heads
    tok = pl.BlockSpec((None, ts, sbw), lambda b, c, pt: (b, 0, 0))

    def page_spec(j):
        return pl.BlockSpec((None, page, sbw),
                            lambda b, c, pt, j=j: (pt[b, npages - 1 - npg * c - j], 0, 0))

    pages = [page_spec(j) for j in range(npg)]
    grid_spec = pltpu.PrefetchScalarGridSpec(
        num_scalar_prefetch=1,
        grid=(nb, npages // npg),
        in_specs=[tok, tok, tok, pl.BlockSpec(bias_col.shape, lambda b, c, pt: (0, 0))] + pages + pages,
        out_specs=tok,
        scratch_shapes=[pltpu.VMEM((rows, sbw), BF16), pltpu.VMEM((page, sbw), F32),
                        pltpu.VMEM((rows, sbw), F32), pltpu.VMEM((rows, 1), F32)],
    )
    return pl.pallas_call(
        functools.partial(_sattn_kernel, npg=npg, heads=heads, dh=dh),
        grid_spec=grid_spec,
        out_shape=jax.ShapeDtypeStruct((nb, ts, sbw), F32),
        compiler_params=_params("arbitrary", "arbitrary"),
        name="sattn",
    )(page_table, q3, kn3, vn3, bias_col, *([ck] * npg), *([cv] * npg))


def _outproj_kernel(x_ref, gate_ref, att_ref, gsb_ref, ssm_ref, w_ref, fg_ref, o_ref):
    sbw = att_ref.shape[-1]
    mixed = (_dot((att_ref[...] * gsb_ref[...]).astype(BF16), w_ref[:sbw, :])
             + _dot(ssm_ref[...].astype(BF16), w_ref[sbw:, :]))
    y = x_ref[...] + gate_ref[...] * mixed
    ms = jnp.mean(y * y, axis=-1, keepdims=True)
    o_ref[...] = y * lax.rsqrt(ms + NORM_EPS) * fg_ref[...]


def _outproj_prompt(x, m3, att, gsb, ssm, w_out, fg):
    b, t, d = x.shape
    sbw, ssw = att.shape[-1], ssm.shape[-1]
    tm = min(PROMPT_ROWS, t)

    def row(w):
        return pl.BlockSpec((None, tm, w), lambda i, j: (i, j, 0))

    return pl.pallas_call(
        _outproj_kernel,
        grid=(b, t // tm),
        in_specs=[row(d), pl.BlockSpec((None, 1, d), lambda i, j: (i, 0, 2)),
                  row(sbw), row(sbw), row(ssw), _const(w_out.shape), _const(fg.shape)],
        out_specs=row(d),
        out_shape=jax.ShapeDtypeStruct((b, t, d), F32),
        compiler_params=_params("arbitrary", "arbitrary"),
        name="outproj_prompt",
    )(x, m3, att, gsb, ssm, w_out, fg)


def _outproj_sample(x2, m, att2, gsb2, ssm2, w_out, fg, ts):
    nb = x2.shape[0]
    d = x2.shape[1] // ts
    sbw, ssw = att2.shape[1] // ts, ssm2.shape[1] // ts

    def col(w):
        return pl.BlockSpec((nb, w), lambda t: (0, t))

    return pl.pallas_call(
        _outproj_kernel,
        grid=(ts,),
        in_specs=[col(d), pl.BlockSpec((nb, d), lambda t: (0, 2)),
                  col(sbw), col(sbw), col(ssw), _const(w_out.shape), _const(fg.shape)],
        out_specs=col(d),
        out_shape=jax.ShapeDtypeStruct((nb, ts * d), F32),
        compiler_params=_params("arbitrary"),
        name="outproj_sample",
    )(x2, m, att2, gsb2, ssm2, w_out, fg)


def _block_diag(w):
    g, a, b = w.shape
    eye = jnp.eye(g, dtype=w.dtype)
    return (eye[:, None, :, None] * w[:, :, None, :]).reshape(g * a, g * b)


def kernel(x_prompt, x_sample, c_prompt, c_sample, cache_k, cache_v, page_table, state_ssm_re, state_ssm_im, norm_g, w_ada, b_ada, w_in, sb_bias, ssm_a_re, ssm_a_im, ssm_log_dt, ssm_b_re, ssm_b_im, ssm_c_re, ssm_c_im, ssm_d, w_glu, b_glu, w_out, final_norm_g):
    depth = w_in.shape[0]
    assert depth == 1, "single mixer layer"
    b, t, d = x_prompt.shape
    nb, ts, _ = x_sample.shape
    heads = sb_bias.shape[1]
    dh = cache_k.shape[-1]
    sbw = heads * dh
    n_pool, page = cache_k.shape[1], cache_k.shape[2]
    g, p = ssm_a_re.shape[1:]
    ch = ssm_b_re.shape[-1]
    gp, ssw = g * p, g * ch
    qscale = float(dh) ** -0.5

    w_in_b = w_in[0].astype(BF16)
    w_out_b = w_out[0].astype(BF16)
    w_glu_b = w_glu[0].astype(BF16)
    ng = norm_g[0].reshape(1, d)
    fg = final_norm_g.reshape(1, d)
    bglu = b_glu[0].reshape(1, -1)
    dsk = ssm_d[0].reshape(1, ssw)
    are = ssm_a_re[0].reshape(1, gp)
    aim = ssm_a_im[0].reshape(1, gp)
    ldt = jnp.repeat(ssm_log_dt[0], p).reshape(1, gp)
    bre_bd = _block_diag(jnp.swapaxes(ssm_b_re[0], 1, 2))
    bim_bd = _block_diag(jnp.swapaxes(ssm_b_im[0], 1, 2))
    cre_bd = _block_diag(jnp.swapaxes(ssm_c_re[0], 1, 2)).astype(BF16)
    cim_bd = _block_diag(jnp.swapaxes(ssm_c_im[0], 1, 2)).astype(BF16)

    m = _ada(jnp.concatenate([c_prompt, c_sample], axis=0), w_ada[0], b_ada[0].reshape(1, -1))
    m_p = m[:b].reshape(b, 1, 3 * d)
    m_s = m[b:]
    coef, wb = _ssm_param(are, aim, ldt, bre_bd, bim_bd)

    q, k_p, v_p, gsb, u, gssm = _inproj_prompt(x_prompt, ng, m_p, w_in_b, sbw, qscale)
    att = _pattn(sb_bias[0], q, k_p, v_p, dh)
    ssm, hre_p, him_p = _pssm(u, gssm, wb, coef, cre_bd, cim_bd, dsk, w_glu_b, bglu)
    y_p = _outproj_prompt(x_prompt, m_p, att, gsb, ssm, w_out_b, fg)

    x2 = x_sample.reshape(nb, ts * d)
    q2, k2, v2, gsb2, u2, gssm2 = _inproj_sample(x2, ng, m_s, w_in_b, sbw, qscale, ts)
    bias_col = jnp.tile(sb_bias[0], ts * min(SAMPLE_PAGES, page_table.shape[1])).reshape(-1, 1)
    att_s = _sattn(page_table, q2.reshape(nb, ts, sbw), k2.reshape(nb, ts, sbw), v2.reshape(nb, ts, sbw),
                   bias_col, cache_k[0].reshape(n_pool, page, sbw), cache_v[0].reshape(n_pool, page, sbw),
                   heads, dh)
    ssm2, hre_s, him_s = _sssm(u2, gssm2, state_ssm_re[0].reshape(nb, gp), state_ssm_im[0].reshape(nb, gp),
                               wb, coef, cre_bd, cim_bd, dsk, w_glu_b, bglu, ts)
    y_s = _outproj_sample(x2, m_s, att_s.reshape(nb, ts * sbw), gsb2, ssm2, w_out_b, fg, ts)

    return (y_p, y_s.reshape(nb, ts, d),
            k_p.reshape(1, b, t, heads, dh), v_p.reshape(1, b, t, heads, dh),
            hre_p.reshape(1, b, g, p), him_p.reshape(1, b, g, p),
            k2.reshape(1, nb, ts, heads, dh), v2.reshape(1, nb, ts, heads, dh),
            hre_s.reshape(1, nb, g, p), him_s.reshape(1, nb, g, p))
```

```python
import functools

import jax
import jax.numpy as jnp
from jax import lax
from jax.experimental import pallas as pl
from jax.experimental.pallas import tpu as pltpu

F32 = jnp.float32
BF16 = jnp.bfloat16
NORM_EPS = 1e-6
SUBLANES = 8
VMEM_LIMIT = 48 * 1024 * 1024

PROMPT_ROWS = 512
ATTN_BLOCK = 256
SSM_ROWS = 256
SCAN_LANES = 512
SAMPLE_PAGES = 8
CHUNK_LANES = 256
MASKED = 1e30
LOG2E = 1.4426950408889634


def _params(*sem):
    return pltpu.CompilerParams(dimension_semantics=sem, vmem_limit_bytes=VMEM_LIMIT)


def _silu(x):
    return x * jax.nn.sigmoid(x)


def _dot(a, b):
    return jnp.dot(a, b, preferred_element_type=F32)


def _dot_nt(a, b):
    return lax.dot_general(a, b, (((1,), (1,)), ((), ())), preferred_element_type=F32)


def _ada_kernel(c_ref, w_ref, b_ref, o_ref):
    c = c_ref[...]
    o_ref[...] = _dot(_silu(c).astype(BF16), w_ref[...].astype(BF16)) + b_ref[...]


def _ada(c, w, b):
    n, d = c.shape
    n3 = w.shape[1]
    return pl.pallas_call(
        _ada_kernel,
        grid=(n3 // d,),
        in_specs=[pl.BlockSpec((n, d), lambda j: (0, 0)),
                  pl.BlockSpec((d, d), lambda j: (0, j)),
                  pl.BlockSpec((1, d), lambda j: (0, j))],
        out_specs=pl.BlockSpec((n, d), lambda j: (0, j)),
        out_shape=jax.ShapeDtypeStruct((n, n3), F32),
        compiler_params=_params("arbitrary"),
        name="ada",
    )(c, w, b)


def _cmul(a, b):
    return a[0] * b[0] - a[1] * b[1], a[0] * b[1] + a[1] * b[0]


def _ssm_param_kernel(are_ref, aim_ref, ldt_ref, bre_ref, bim_ref, coef_ref, wb_ref):
    gp = are_ref.shape[1]
    are, aim = are_ref[...], aim_ref[...]
    dt = jnp.exp(ldt_ref[...])
    mag = jnp.exp(dt * are)
    ar = mag * jnp.cos(dt * aim)
    ai = mag * jnp.sin(dt * aim)
    den = are * are + aim * aim
    nre = ar - 1.0
    fre = (nre * are + ai * aim) / den
    fim = (ai * are - nre * aim) / den
    bre, bim = bre_ref[...], bim_ref[...]
    wb_ref[:, :gp] = (fre * bre - fim * bim).astype(BF16)
    wb_ref[:, gp:] = (fre * bim + fim * bre).astype(BF16)

    pw = [(ar, ai)]
    for _ in range(SUBLANES - 1):
        pw.append(_cmul(pw[-1], (ar, ai)))
    row = lax.broadcasted_iota(jnp.int32, (SUBLANES, gp), 0)
    for n, d in enumerate((1, 2, 4)):
        for part in range(2):
            coef_ref[2 * n + part] = jnp.where(row >= d, pw[d - 1][part], 0.0)
    for part in range(2):
        acc = jnp.zeros((SUBLANES, gp), F32)
        for r in range(SUBLANES):
            acc = jnp.where(row == r, pw[r][part], acc)
        coef_ref[6 + part] = acc


def _ssm_param(are, aim, ldt, bre_bd, bim_bd):
    gc, gp = bre_bd.shape
    return pl.pallas_call(
        _ssm_param_kernel,
        out_shape=(jax.ShapeDtypeStruct((8, SUBLANES, gp), F32),
                   jax.ShapeDtypeStruct((gc, 2 * gp), BF16)),
        compiler_params=_params(),
        name="ssm_param",
    )(are, aim, ldt, bre_bd, bim_bd)


def _inproj_kernel(x_ref, ng_ref, shift_ref, scale_ref, w_ref,
                   q_ref, k_ref, v_ref, gsb_ref, u_ref, gssm_ref, *, qscale):
    x = x_ref[...]
    ms = jnp.mean(x * x, axis=-1, keepdims=True)
    h = x * lax.rsqrt(ms + NORM_EPS) * ng_ref[...]
    h = (h * (1.0 + scale_ref[...]) + shift_ref[...]).astype(BF16)
    w = q_ref.shape[-1]

    def proj(c):
        return _dot(h, w_ref[:, c * w:(c + 1) * w])

    q_ref[...] = proj(0) * qscale
    k_ref[...] = proj(1)
    v_ref[...] = proj(2)
    gsb_ref[...] = _silu(proj(3))
    u_ref[...] = proj(4)
    gssm_ref[...] = _silu(proj(5))


def _inproj_prompt(x, ng, m3, w_in, sbw, qscale):
    b, t, d = x.shape
    tm = min(PROMPT_ROWS, t)
    row = pl.BlockSpec((None, tm, d), lambda i, j: (i, j, 0))
    out = pl.BlockSpec((None, tm, sbw), lambda i, j: (i, j, 0))
    return pl.pallas_call(
        functools.partial(_inproj_kernel, qscale=qscale),
        grid=(b, t // tm),
        in_specs=[row,
                  pl.BlockSpec((1, d), lambda i, j: (0, 0)),
                  pl.BlockSpec((None, 1, d), lambda i, j: (i, 0, 0)),
                  pl.BlockSpec((None, 1, d), lambda i, j: (i, 0, 1)),
                  pl.BlockSpec(w_in.shape, lambda i, j: (0, 0))],
        out_specs=[out] * 6,
        out_shape=[jax.ShapeDtypeStruct((b, t, sbw), F32)] * 6,
        compiler_params=_params("arbitrary", "arbitrary"),
        name="inproj_prompt",
    )(x, ng, m3, m3, w_in)


def _inproj_sample(x2, ng, m, w_in, sbw, qscale, ts):
    nb = x2.shape[0]
    d = x2.shape[1] // ts
    out = pl.BlockSpec((nb, sbw), lambda t: (0, t))
    return pl.pallas_call(
        functools.partial(_inproj_kernel, qscale=qscale),
        grid=(ts,),
        in_specs=[pl.BlockSpec((nb, d), lambda t: (0, t)),
                  pl.BlockSpec((1, d), lambda t: (0, 0)),
                  pl.BlockSpec((nb, d), lambda t: (0, 0)),
                  pl.BlockSpec((nb, d), lambda t: (0, 1)),
                  pl.BlockSpec(w_in.shape, lambda t: (0, 0))],
        out_specs=[out] * 6,
        out_shape=[jax.ShapeDtypeStruct((nb, ts * sbw), F32)] * 6,
        compiler_params=_params("arbitrary"),
        name="inproj_sample",
    )(x2, ng, m, m, w_in)


def _pattn_kernel(bias_ref, q_ref, k_ref, v_ref, o_ref, z_ref, e_ref, tot_ref, c_ref, *, dh):
    tq = q_ref.shape[0]
    hp = pl.program_id(1)
    i = pl.program_id(2)
    ntiles = i + 1
    first = lax.broadcasted_iota(jnp.int32, (1, 2 * dh), 1) < dh
    q = q_ref[...]
    nq = (jnp.where(first, -q, 0.0), jnp.where(first, 0.0, -q))
    nbias = (-LOG2E * bias_ref[2 * hp], -LOG2E * bias_ref[2 * hp + 1])
    r = lax.broadcasted_iota(jnp.int32, (tq, tq), 0)
    c = lax.broadcasted_iota(jnp.int32, (tq, tq), 1)
    incl_mat = (r >= c).astype(F32)
    incl_mat2 = jnp.concatenate([incl_mat, incl_mat], axis=0)

    def rows(n):
        return pl.ds(pl.multiple_of((i - n) * tq, tq), tq)

    def stage1(n, masked):
        kb = k_ref[rows(n), :]
        for h in range(2):
            zn = _dot_nt(nq[h], kb) + nbias[h]
            if masked:
                zn = jnp.where(c < r, zn, MASKED)
            z_ref[h] = zn

    def stage2():
        for h in range(2):
            zn = z_ref[h]
            lk = jnp.minimum(zn, 0.0) - LOG2E * jnp.log(1.0 + jnp.exp2(-jnp.abs(zn)))
            hi = lk.astype(BF16).astype(F32)
            incl = _dot(jnp.concatenate([hi, lk - hi], axis=1), incl_mat2)
            e_ref[h] = incl - zn
            tot_ref[h] = incl[:, :tot_ref.shape[-1]]

    def stage3(n):
        vb = v_ref[rows(n), :]
        outs = []
        for h in range(2):
            cc = c_ref[h]
            outs.append(_dot(jnp.exp2(e_ref[h] + cc), vb))
            c_ref[h] = cc + tot_ref[h][:, :1]
        o_ref[...] += jnp.where(first, outs[0], outs[1])

    o_ref[...] = jnp.zeros_like(o_ref)
    c_ref[...] = jnp.zeros_like(c_ref)
    stage1(0, True)

    @pl.when(ntiles == 1)
    def _():
        stage2()

    @pl.when(ntiles >= 2)
    def _():
        stage2()
        stage1(1, False)

    def steady(n, carry):
        stage3(n - 2)
        stage2()
        stage1(n, False)
        return carry

    lax.fori_loop(2, ntiles, steady, 0)

    @pl.when(ntiles >= 2)
    def _():
        stage3(ntiles - 2)
        stage2()

    stage3(ntiles - 1)


def _pattn(bias, q, k, v, dh):
    b, t, sbw = q.shape
    tq = min(ATTN_BLOCK, t)
    blk = pl.BlockSpec((None, tq, 2 * dh), lambda bi, hp, i: (bi, i, hp))
    full = pl.BlockSpec((None, t, 2 * dh), lambda bi, hp, i: (bi, 0, hp))
    return pl.pallas_call(
        functools.partial(_pattn_kernel, dh=dh),
        grid=(b, sbw // (2 * dh), t // tq),
        in_specs=[pl.BlockSpec(memory_space=pltpu.SMEM), blk, full, full],
        out_specs=blk,
        out_shape=jax.ShapeDtypeStruct((b, t, sbw), F32),
        scratch_shapes=[pltpu.VMEM((2, tq, tq), F32), pltpu.VMEM((2, tq, tq), F32),
                        pltpu.VMEM((2, tq, 2 * dh), F32), pltpu.VMEM((2, tq, 1), F32)],
        compiler_params=_params("arbitrary", "arbitrary", "arbitrary"),
        name="pattn",
    )(bias, q, k, v)


def _ssm_readout(hre, him, u, gs, cre_ref, cim_ref, d_ref, wglu_ref, bglu_ref):
    y = _dot(hre.astype(BF16), cre_ref[...]) - _dot(him.astype(BF16), cim_ref[...]) + d_ref[...] * u
    y = jax.nn.gelu(y)
    z = _dot(y.astype(BF16), wglu_ref[...]) + bglu_ref[...]
    w = z.shape[-1] // 2
    return z[:, :w] * jax.nn.sigmoid(z[:, w:]) * gs


def _pssm_kernel(u_ref, gs_ref, wb_ref, coef_ref, cre_ref, cim_ref, d_ref, wglu_ref, bglu_ref,
                 o_ref, hre_ref, him_ref, st_ref, carry_ref):
    tt = u_ref.shape[0]
    gp = cre_ref.shape[0]
    j = pl.program_id(1)

    @pl.when(j == 0)
    def _():
        carry_ref[...] = jnp.zeros_like(carry_ref)

    u = u_ref[...]
    st_ref[...] = _dot(u.astype(BF16), wb_ref[...])

    for lc in range(gp // SCAN_LANES):
        lre = pl.ds(lc * SCAN_LANES, SCAN_LANES)
        lim = pl.ds(gp + lc * SCAN_LANES, SCAN_LANES)

        def group(g, carry, lre=lre, lim=lim):
            cr, ci = carry
            rows = pl.ds(pl.multiple_of(g * SUBLANES, SUBLANES), SUBLANES)
            xr = st_ref[rows, lre]
            xi = st_ref[rows, lim]
            for n, d in enumerate((1, 2, 4)):
                ar = coef_ref[2 * n, :, lre]
                ai = coef_ref[2 * n + 1, :, lre]
                sr = pltpu.roll(xr, d, 0)
                si = pltpu.roll(xi, d, 0)
                xr, xi = xr + (ar * sr - ai * si), xi + (ar * si + ai * sr)
            pr = coef_ref[6, :, lre]
            pi = coef_ref[7, :, lre]
            hr = xr + (pr * cr - pi * ci)
            hi = xi + (pr * ci + pi * cr)
            st_ref[rows, lre] = hr
            st_ref[rows, lim] = hi
            return hr[SUBLANES - 1:, :], hi[SUBLANES - 1:, :]

        cr, ci = lax.fori_loop(0, tt // SUBLANES, group, (carry_ref[:, lre], carry_ref[:, lim]))
        carry_ref[:, lre] = cr
        carry_ref[:, lim] = ci

    o_ref[...] = _ssm_readout(st_ref[:, :gp], st_ref[:, gp:], u, gs_ref[...],
                              cre_ref, cim_ref, d_ref, wglu_ref, bglu_ref)
    hre_ref[...] = carry_ref[:, :gp]
    him_ref[...] = carry_ref[:, gp:]


def _const(shape):
    nd = len(shape)
    return pl.BlockSpec(shape, lambda *_: (0,) * nd)


def _pssm(u, gs, wb, coef, cre, cim, dsk, wglu, bglu):
    b, t, ssw = u.shape
    gp = cre.shape[0]
    tt = min(SSM_ROWS, t)
    row = pl.BlockSpec((None, tt, ssw), lambda i, j: (i, j, 0))
    st = pl.BlockSpec((None, 1, gp), lambda i, j: (i, 0, 0))
    return pl.pallas_call(
        _pssm_kernel,
        grid=(b, t // tt),
        in_specs=[row, row, _const(wb.shape), _const(coef.shape), _const(cre.shape), _const(cim.shape),
                  _const(dsk.shape), _const(wglu.shape), _const(bglu.shape)],
        out_specs=[row, st, st],
        out_shape=[jax.ShapeDtypeStruct((b, t, ssw), F32),
                   jax.ShapeDtypeStruct((b, 1, gp), F32),
                   jax.ShapeDtypeStruct((b, 1, gp), F32)],
        scratch_shapes=[pltpu.VMEM((tt, 2 * gp), F32), pltpu.VMEM((1, 2 * gp), F32)],
        compiler_params=_params("arbitrary", "arbitrary"),
        name="pssm",
    )(u, gs, wb, coef, cre, cim, dsk, wglu, bglu)


def _sssm_kernel(u_ref, gs_ref, h0re_ref, h0im_ref, wb_ref, coef_ref, cre_ref, cim_ref, d_ref,
                 wglu_ref, bglu_ref, o_ref, hre_ref, him_ref):
    gp = cre_ref.shape[0]
    t = pl.program_id(0)

    @pl.when(t == 0)
    def _():
        hre_ref[...] = h0re_ref[...]
        him_ref[...] = h0im_ref[...]

    u = u_ref[...]
    bu = _dot(u.astype(BF16), wb_ref[...])
    ar = coef_ref[6, 0:1, :]
    ai = coef_ref[7, 0:1, :]
    hr, hi = hre_ref[...], him_ref[...]
    nr = ar * hr - ai * hi + bu[:, :gp]
    ni = ar * hi + ai * hr + bu[:, gp:]
    hre_ref[...] = nr
    him_ref[...] = ni
    o_ref[...] = _ssm_readout(nr, ni, u, gs_ref[...], cre_ref, cim_ref, d_ref, wglu_ref, bglu_ref)


def _sssm(u2, gs2, h0re, h0im, wb, coef, cre, cim, dsk, wglu, bglu, ts):
    nb = u2.shape[0]
    ssw = u2.shape[1] // ts
    gp = cre.shape[0]
    row = pl.BlockSpec((nb, ssw), lambda t: (0, t))
    return pl.pallas_call(
        _sssm_kernel,
        grid=(ts,),
        in_specs=[row, row, _const(h0re.shape), _const(h0im.shape), _const(wb.shape), _const(coef.shape),
                  _const(cre.shape), _const(cim.shape), _const(dsk.shape), _const(wglu.shape),
                  _const(bglu.shape)],
        out_specs=[row, _const((nb, gp)), _const((nb, gp))],
        out_shape=[jax.ShapeDtypeStruct((nb, ts * ssw), F32),
                   jax.ShapeDtypeStruct((nb, gp), F32),
                   jax.ShapeDtypeStruct((nb, gp), F32)],
        compiler_params=_params("arbitrary"),
        name="sssm",
    )(u2, gs2, h0re, h0im, wb, coef, cre, cim, dsk, wglu, bglu)


def _sattn_kernel(pt_ref, q_ref, kn_ref, vn_ref, bias_ref, *refs, npg):
    kp, vp = refs[:npg], refs[npg:2 * npg]
    o_ref, pad_ref, c_ref = refs[2 * npg:]
    del pt_ref
    rows, dh = q_ref.shape
    page, heads, _ = kp[0].shape
    nck = page * heads // CHUNK_LANES
    step = pl.program_id(1)
    nq = -q_ref[...]
    nbias = -bias_ref[...]
    lane = lax.broadcasted_iota(jnp.int32, (rows, CHUNK_LANES), 1)
    row = lax.broadcasted_iota(jnp.int32, (rows, CHUNK_LANES), 0)
    same_head = lane % heads == row % heads
    kr = lax.broadcasted_iota(jnp.int32, (CHUNK_LANES, CHUNK_LANES), 0) // heads
    kc = lax.broadcasted_iota(jnp.int32, (CHUNK_LANES, CHUNK_LANES), 1) // heads
    incl_mat = (kr >= kc).astype(F32)

    def suffix(zn):
        lk = jnp.minimum(zn, 0.0) - jnp.log(1.0 + jnp.exp(-jnp.abs(zn)))
        hi = lk.astype(BF16).astype(F32)
        incl = _dot(hi, incl_mat) + _dot(lk - hi, incl_mat)
        return incl - zn, incl[:, :1]

    @pl.when(step == 0)
    def _():
        pad_ref[...] = jnp.zeros_like(pad_ref)
        pad_ref[0:rows, :] = kn_ref[...]
        zt = _dot_nt(nq, pad_ref[...]) + nbias
        e, tot = suffix(jnp.where(same_head & (lane // heads < row // heads), zt, MASKED))
        pad_ref[0:rows, :] = vn_ref[...]
        o_ref[...] = _dot(jnp.exp(e), pad_ref[...])
        c_ref[...] = tot

    pieces = []
    for j in range(npg):
        zt = _dot_nt(nq, kp[j][...].reshape(page * heads, dh)) + nbias
        for k in reversed(range(nck)):
            pieces.append(jnp.where(same_head, zt[:, k * CHUNK_LANES:(k + 1) * CHUNK_LANES], MASKED))
    e, tot = suffix(jnp.concatenate(pieces, axis=0))
    run = c_ref[...]
    offs = []
    for m in range(npg * nck):
        offs.append(run)
        run = run + tot[m * rows:(m + 1) * rows]
    c_ref[...] = run
    a = jnp.exp(e + jnp.concatenate(offs, axis=0))
    acc = o_ref[...]
    for j in range(npg):
        aj = jnp.concatenate([a[(j * nck + nck - 1 - k) * rows:(j * nck + nck - k) * rows]
                              for k in range(nck)], axis=1)
        acc = acc + _dot(aj, vp[j][...].reshape(page * heads, dh))
    o_ref[...] = acc


def _sattn(page_table, q3, kn3, vn3, bias_col, ck, cv):
    nb, rows, dh = q3.shape
    npages = page_table.shape[1]
    page, heads = ck.shape[2], ck.shape[3]
    npg = min(SAMPLE_PAGES, npages)
    tok = pl.BlockSpec((None, rows, dh), lambda b, c, pt: (b, 0, 0))

    def page_spec(j):
        return pl.BlockSpec((None, None, page, heads, dh),
                            lambda b, c, pt, j=j: (0, pt[b, npages - 1 - npg * c - j], 0, 0, 0))

    pages = [page_spec(j) for j in range(npg)]
    grid_spec = pltpu.PrefetchScalarGridSpec(
        num_scalar_prefetch=1,
        grid=(nb, npages // npg),
        in_specs=[tok, tok, tok, pl.BlockSpec(bias_col.shape, lambda b, c, pt: (0, 0))] + pages + pages,
        out_specs=tok,
        scratch_shapes=[pltpu.VMEM((CHUNK_LANES, dh), F32), pltpu.VMEM((rows, 1), F32)],
    )
    return pl.pallas_call(
        functools.partial(_sattn_kernel, npg=npg),
        grid_spec=grid_spec,
        out_shape=jax.ShapeDtypeStruct((nb, rows, dh), F32),
        compiler_params=_params("arbitrary", "arbitrary"),
        name="sattn",
    )(page_table, q3, kn3, vn3, bias_col, *([ck] * npg), *([cv] * npg))


def _outproj_kernel(x_ref, gate_ref, att_ref, gsb_ref, ssm_ref, w_ref, fg_ref, o_ref):
    sbw = att_ref.shape[-1]
    mixed = (_dot((att_ref[...] * gsb_ref[...]).astype(BF16), w_ref[:sbw, :])
             + _dot(ssm_ref[...].astype(BF16), w_ref[sbw:, :]))
    y = x_ref[...] + gate_ref[...] * mixed
    ms = jnp.mean(y * y, axis=-1, keepdims=True)
    o_ref[...] = y * lax.rsqrt(ms + NORM_EPS) * fg_ref[...]


def _outproj_prompt(x, m3, att, gsb, ssm, w_out, fg):
    b, t, d = x.shape
    sbw, ssw = att.shape[-1], ssm.shape[-1]
    tm = min(PROMPT_ROWS, t)

    def row(w):
        return pl.BlockSpec((None, tm, w), lambda i, j: (i, j, 0))

    return pl.pallas_call(
        _outproj_kernel,
        grid=(b, t // tm),
        in_specs=[row(d), pl.BlockSpec((None, 1, d), lambda i, j: (i, 0, 2)),
                  row(sbw), row(sbw), row(ssw), _const(w_out.shape), _const(fg.shape)],
        out_specs=row(d),
        out_shape=jax.ShapeDtypeStruct((b, t, d), F32),
        compiler_params=_params("arbitrary", "arbitrary"),
        name="outproj_prompt",
    )(x, m3, att, gsb, ssm, w_out, fg)


def _outproj_sample(x2, m, att2, gsb2, ssm2, w_out, fg, ts):
    nb = x2.shape[0]
    d = x2.shape[1] // ts
    sbw, ssw = att2.shape[1] // ts, ssm2.shape[1] // ts

    def col(w):
        return pl.BlockSpec((nb, w), lambda t: (0, t))

    return pl.pallas_call(
        _outproj_kernel,
        grid=(ts,),
        in_specs=[col(d), pl.BlockSpec((nb, d), lambda t: (0, 2)),
                  col(sbw), col(sbw), col(ssw), _const(w_out.shape), _const(fg.shape)],
        out_specs=col(d),
        out_shape=jax.ShapeDtypeStruct((nb, ts * d), F32),
        compiler_params=_params("arbitrary"),
        name="outproj_sample",
    )(x2, m, att2, gsb2, ssm2, w_out, fg)


def _block_diag(w):
    g, a, b = w.shape
    eye = jnp.eye(g, dtype=w.dtype)
    return (eye[:, None, :, None] * w[:, :, None, :]).reshape(g * a, g * b)


def kernel(x_prompt, x_sample, c_prompt, c_sample, cache_k, cache_v, page_table, state_ssm_re, state_ssm_im, norm_g, w_ada, b_ada, w_in, sb_bias, ssm_a_re, ssm_a_im, ssm_log_dt, ssm_b_re, ssm_b_im, ssm_c_re, ssm_c_im, ssm_d, w_glu, b_glu, w_out, final_norm_g):
    depth = w_in.shape[0]
    assert depth == 1, "single mixer layer"
    b, t, d = x_prompt.shape
    nb, ts, _ = x_sample.shape
    heads = sb_bias.shape[1]
    dh = cache_k.shape[-1]
    sbw = heads * dh
    n_pool, page = cache_k.shape[1], cache_k.shape[2]
    g, p = ssm_a_re.shape[1:]
    ch = ssm_b_re.shape[-1]
    gp, ssw = g * p, g * ch
    qscale = float(dh) ** -0.5

    w_in_b = w_in[0].astype(BF16)
    w_out_b = w_out[0].astype(BF16)
    w_glu_b = w_glu[0].astype(BF16)
    ng = norm_g[0].reshape(1, d)
    fg = final_norm_g.reshape(1, d)
    bglu = b_glu[0].reshape(1, -1)
    dsk = ssm_d[0].reshape(1, ssw)
    are = ssm_a_re[0].reshape(1, gp)
    aim = ssm_a_im[0].reshape(1, gp)
    ldt = jnp.repeat(ssm_log_dt[0], p).reshape(1, gp)
    bre_bd = _block_diag(jnp.swapaxes(ssm_b_re[0], 1, 2))
    bim_bd = _block_diag(jnp.swapaxes(ssm_b_im[0], 1, 2))
    cre_bd = _block_diag(jnp.swapaxes(ssm_c_re[0], 1, 2)).astype(BF16)
    cim_bd = _block_diag(jnp.swapaxes(ssm_c_im[0], 1, 2)).astype(BF16)

    m = _ada(jnp.concatenate([c_prompt, c_sample], axis=0), w_ada[0], b_ada[0].reshape(1, -1))
    m_p = m[:b].reshape(b, 1, 3 * d)
    m_s = m[b:]
    coef, wb = _ssm_param(are, aim, ldt, bre_bd, bim_bd)

    q, k_p, v_p, gsb, u, gssm = _inproj_prompt(x_prompt, ng, m_p, w_in_b, sbw, qscale * LOG2E)
    att = _pattn(sb_bias[0], q, k_p, v_p, dh)
    ssm, hre_p, him_p = _pssm(u, gssm, wb, coef, cre_bd, cim_bd, dsk, w_glu_b, bglu)
    y_p = _outproj_prompt(x_prompt, m_p, att, gsb, ssm, w_out_b, fg)

    x2 = x_sample.reshape(nb, ts * d)
    q2, k2, v2, gsb2, u2, gssm2 = _inproj_sample(x2, ng, m_s, w_in_b, sbw, qscale, ts)
    bias_col = jnp.tile(sb_bias[0], ts).reshape(ts * heads, 1)
    att_s = _sattn(page_table, q2.reshape(nb, ts * heads, dh), k2.reshape(nb, ts * heads, dh),
                   v2.reshape(nb, ts * heads, dh), bias_col, cache_k, cache_v)
    ssm2, hre_s, him_s = _sssm(u2, gssm2, state_ssm_re[0].reshape(nb, gp), state_ssm_im[0].reshape(nb, gp),
                               wb, coef, cre_bd, cim_bd, dsk, w_glu_b, bglu, ts)
    y_s = _outproj_sample(x2, m_s, att_s.reshape(nb, ts * sbw), gsb2, ssm2, w_out_b, fg, ts)

    return (y_p, y_s.reshape(nb, ts, d),
            k_p.reshape(1, b, t, heads, dh), v_p.reshape(1, b, t, heads, dh),
            hre_p.reshape(1, b, g, p), him_p.reshape(1, b, g, p),
            k2.reshape(1, nb, ts, heads, dh), v2.reshape(1, nb, ts, heads, dh),
            hre_s.reshape(1, nb, g, p), him_s.reshape(1, nb, g, p))
```

```python
import functools

import jax
import jax.numpy as jnp
from jax import lax
from jax.experimental import pallas as pl
from jax.experimental.pallas import tpu as pltpu

F32 = jnp.float32
BF16 = jnp.bfloat16
NORM_EPS = 1e-6
SUBLANES = 8
VMEM_LIMIT = 48 * 1024 * 1024

PROMPT_ROWS = 512
ATTN_BLOCK = 256
SSM_ROWS = 256
SCAN_LANES = 512
SAMPLE_PAGES = 8
MASKED = 1e30
LOG2E = 1.4426950408889634


def _params(*sem):
    return pltpu.CompilerParams(dimension_semantics=sem, vmem_limit_bytes=VMEM_LIMIT)


def _silu(x):
    return x * jax.nn.sigmoid(x)


def _dot(a, b):
    return jnp.dot(a, b, preferred_element_type=F32)


def _dot_nt(a, b):
    return lax.dot_general(a, b, (((1,), (1,)), ((), ())), preferred_element_type=F32)


def _ada_kernel(c_ref, w_ref, b_ref, o_ref):
    c = c_ref[...]
    o_ref[...] = _dot(_silu(c).astype(BF16), w_ref[...].astype(BF16)) + b_ref[...]


def _ada(c, w, b):
    n, d = c.shape
    n3 = w.shape[1]
    return pl.pallas_call(
        _ada_kernel,
        grid=(n3 // d,),
        in_specs=[pl.BlockSpec((n, d), lambda j: (0, 0)),
                  pl.BlockSpec((d, d), lambda j: (0, j)),
                  pl.BlockSpec((1, d), lambda j: (0, j))],
        out_specs=pl.BlockSpec((n, d), lambda j: (0, j)),
        out_shape=jax.ShapeDtypeStruct((n, n3), F32),
        compiler_params=_params("arbitrary"),
        name="ada",
    )(c, w, b)


def _cmul(a, b):
    return a[0] * b[0] - a[1] * b[1], a[0] * b[1] + a[1] * b[0]


def _ssm_param_kernel(are_ref, aim_ref, ldt_ref, bre_ref, bim_ref, coef_ref, wb_ref):
    gp = are_ref.shape[1]
    are, aim = are_ref[...], aim_ref[...]
    dt = jnp.exp(ldt_ref[...])
    mag = jnp.exp(dt * are)
    ar = mag * jnp.cos(dt * aim)
    ai = mag * jnp.sin(dt * aim)
    den = are * are + aim * aim
    nre = ar - 1.0
    fre = (nre * are + ai * aim) / den
    fim = (ai * are - nre * aim) / den
    bre, bim = bre_ref[...], bim_ref[...]
    wb_ref[:, :gp] = (fre * bre - fim * bim).astype(BF16)
    wb_ref[:, gp:] = (fre * bim + fim * bre).astype(BF16)

    pw = [(ar, ai)]
    for _ in range(SUBLANES - 1):
        pw.append(_cmul(pw[-1], (ar, ai)))
    row = lax.broadcasted_iota(jnp.int32, (SUBLANES, gp), 0)
    for n, d in enumerate((1, 2, 4)):
        for part in range(2):
            coef_ref[2 * n + part] = jnp.where(row >= d, pw[d - 1][part], 0.0)
    for part in range(2):
        acc = jnp.zeros((SUBLANES, gp), F32)
        for r in range(SUBLANES):
            acc = jnp.where(row == r, pw[r][part], acc)
        coef_ref[6 + part] = acc


def _ssm_param(are, aim, ldt, bre_bd, bim_bd):
    gc, gp = bre_bd.shape
    return pl.pallas_call(
        _ssm_param_kernel,
        out_shape=(jax.ShapeDtypeStruct((8, SUBLANES, gp), F32),
                   jax.ShapeDtypeStruct((gc, 2 * gp), BF16)),
        compiler_params=_params(),
        name="ssm_param",
    )(are, aim, ldt, bre_bd, bim_bd)


def _inproj_kernel(x_ref, ng_ref, shift_ref, scale_ref, w_ref,
                   q_ref, k_ref, v_ref, gsb_ref, u_ref, gssm_ref, *, qscale):
    x = x_ref[...]
    ms = jnp.mean(x * x, axis=-1, keepdims=True)
    h = x * lax.rsqrt(ms + NORM_EPS) * ng_ref[...]
    h = (h * (1.0 + scale_ref[...]) + shift_ref[...]).astype(BF16)
    w = q_ref.shape[-1]

    def proj(c):
        return _dot(h, w_ref[:, c * w:(c + 1) * w])

    q_ref[...] = proj(0) * qscale
    k_ref[...] = proj(1)
    v_ref[...] = proj(2)
    gsb_ref[...] = _silu(proj(3))
    u_ref[...] = proj(4)
    gssm_ref[...] = _silu(proj(5))


def _inproj_prompt(x, ng, m3, w_in, sbw, qscale):
    b, t, d = x.shape
    tm = min(PROMPT_ROWS, t)
    row = pl.BlockSpec((None, tm, d), lambda i, j: (i, j, 0))
    out = pl.BlockSpec((None, tm, sbw), lambda i, j: (i, j, 0))
    return pl.pallas_call(
        functools.partial(_inproj_kernel, qscale=qscale),
        grid=(b, t // tm),
        in_specs=[row,
                  pl.BlockSpec((1, d), lambda i, j: (0, 0)),
                  pl.BlockSpec((None, 1, d), lambda i, j: (i, 0, 0)),
                  pl.BlockSpec((None, 1, d), lambda i, j: (i, 0, 1)),
                  pl.BlockSpec(w_in.shape, lambda i, j: (0, 0))],
        out_specs=[out] * 6,
        out_shape=[jax.ShapeDtypeStruct((b, t, sbw), F32)] * 6,
        compiler_params=_params("arbitrary", "arbitrary"),
        name="inproj_prompt",
    )(x, ng, m3, m3, w_in)


def _inproj_sample(x2, ng, m, w_in, sbw, qscale, ts):
    nb = x2.shape[0]
    d = x2.shape[1] // ts
    out = pl.BlockSpec((nb, sbw), lambda t: (0, t))
    return pl.pallas_call(
        functools.partial(_inproj_kernel, qscale=qscale),
        grid=(ts,),
        in_specs=[pl.BlockSpec((nb, d), lambda t: (0, t)),
                  pl.BlockSpec((1, d), lambda t: (0, 0)),
                  pl.BlockSpec((nb, d), lambda t: (0, 0)),
                  pl.BlockSpec((nb, d), lambda t: (0, 1)),
                  pl.BlockSpec(w_in.shape, lambda t: (0, 0))],
        out_specs=[out] * 6,
        out_shape=[jax.ShapeDtypeStruct((nb, ts * sbw), F32)] * 6,
        compiler_params=_params("arbitrary"),
        name="inproj_sample",
    )(x2, ng, m, m, w_in)


def _pattn_kernel(bias_ref, q_ref, k_ref, v_ref, o_ref, z_ref, e_ref, tot_ref, c_ref, *, dh):
    tq = q_ref.shape[0]
    hp = pl.program_id(1)
    i = pl.program_id(2)
    ntiles = i + 1
    first = lax.broadcasted_iota(jnp.int32, (1, 2 * dh), 1) < dh
    q = q_ref[...]
    nq = (jnp.where(first, -q, 0.0), jnp.where(first, 0.0, -q))
    nbias = (-LOG2E * bias_ref[2 * hp], -LOG2E * bias_ref[2 * hp + 1])
    r = lax.broadcasted_iota(jnp.int32, (tq, tq), 0)
    c = lax.broadcasted_iota(jnp.int32, (tq, tq), 1)
    incl_mat = (r >= c).astype(F32)
    incl_mat2 = jnp.concatenate([incl_mat, incl_mat], axis=0)

    def rows(n):
        return pl.ds(pl.multiple_of((i - n) * tq, tq), tq)

    def stage1(n, masked=False):
        kb = k_ref[rows(n), :]
        for h in range(2):
            zn = _dot_nt(nq[h], kb) + nbias[h]
            if masked:
                zn = jnp.where(c < r, zn, MASKED)
            z_ref[h] = zn

    def stage2():
        for h in range(2):
            zn = z_ref[h]
            lk = jnp.minimum(zn, 0.0) - LOG2E * jnp.log(1.0 + jnp.exp2(-jnp.abs(zn)))
            hi = lk.astype(BF16).astype(F32)
            incl = _dot(jnp.concatenate([hi, lk - hi], axis=1), incl_mat2)
            e_ref[h] = incl - zn
            tot_ref[h] = incl[:, :tot_ref.shape[-1]]

    def stage3(n):
        vb = v_ref[rows(n), :]
        outs = []
        for h in range(2):
            cc = c_ref[h]
            outs.append(_dot(jnp.exp2(e_ref[h] + jnp.concatenate([cc] * (tq // cc.shape[1]), axis=1)), vb))
            c_ref[h] = cc + tot_ref[h][:, :1]
        o_ref[...] += jnp.where(first, outs[0], outs[1])

    o_ref[...] = jnp.zeros_like(o_ref)
    c_ref[...] = jnp.zeros_like(c_ref)
    stage1(0, masked=True)

    @pl.when(ntiles == 1)
    def _():
        stage2()

    @pl.when(ntiles >= 2)
    def _():
        stage2()
        stage1(1)

    def steady(n, carry):
        stage3(n - 2)
        stage2()
        stage1(n)
        return carry

    lax.fori_loop(2, ntiles, steady, 0)

    @pl.when(ntiles >= 2)
    def _():
        stage3(ntiles - 2)
        stage2()

    stage3(ntiles - 1)


def _pattn(bias, q, k, v, dh):
    b, t, sbw = q.shape
    tq = min(ATTN_BLOCK, t)
    blk = pl.BlockSpec((None, tq, 2 * dh), lambda bi, hp, i: (bi, i, hp))
    full = pl.BlockSpec((None, t, 2 * dh), lambda bi, hp, i: (bi, 0, hp))
    return pl.pallas_call(
        functools.partial(_pattn_kernel, dh=dh),
        grid=(b, sbw // (2 * dh), t // tq),
        in_specs=[pl.BlockSpec(memory_space=pltpu.SMEM), blk, full, full],
        out_specs=blk,
        out_shape=jax.ShapeDtypeStruct((b, t, sbw), F32),
        scratch_shapes=[pltpu.VMEM((2, tq, tq), F32), pltpu.VMEM((2, tq, tq), F32),
                        pltpu.VMEM((2, tq, 2 * dh), F32), pltpu.VMEM((2, tq, 2 * dh), F32)],
        compiler_params=_params("arbitrary", "arbitrary", "arbitrary"),
        name="pattn",
    )(bias, q, k, v)


def _ssm_readout(hre, him, u, gs, cre_ref, cim_ref, d_ref, wglu_ref, bglu_ref):
    y = _dot(hre.astype(BF16), cre_ref[...]) - _dot(him.astype(BF16), cim_ref[...]) + d_ref[...] * u
    y = jax.nn.gelu(y)
    z = _dot(y.astype(BF16), wglu_ref[...]) + bglu_ref[...]
    w = z.shape[-1] // 2
    return z[:, :w] * jax.nn.sigmoid(z[:, w:]) * gs


def _pssm_kernel(u_ref, gs_ref, wb_ref, coef_ref, cre_ref, cim_ref, d_ref, wglu_ref, bglu_ref,
                 o_ref, hre_ref, him_ref, st_ref, carry_ref):
    tt = u_ref.shape[0]
    gp = cre_ref.shape[0]
    j = pl.program_id(1)

    @pl.when(j == 0)
    def _():
        carry_ref[...] = jnp.zeros_like(carry_ref)

    u = u_ref[...]
    st_ref[...] = _dot(u.astype(BF16), wb_ref[...])

    for lc in range(gp // SCAN_LANES):
        lre = pl.ds(lc * SCAN_LANES, SCAN_LANES)
        lim = pl.ds(gp + lc * SCAN_LANES, SCAN_LANES)

        def group(g, carry, lre=lre, lim=lim):
            cr, ci = carry
            rows = pl.ds(pl.multiple_of(g * SUBLANES, SUBLANES), SUBLANES)
            xr = st_ref[rows, lre]
            xi = st_ref[rows, lim]
            for n, d in enumerate((1, 2, 4)):
                ar = coef_ref[2 * n, :, lre]
                ai = coef_ref[2 * n + 1, :, lre]
                sr = pltpu.roll(xr, d, 0)
                si = pltpu.roll(xi, d, 0)
                xr, xi = xr + (ar * sr - ai * si), xi + (ar * si + ai * sr)
            pr = coef_ref[6, :, lre]
            pi = coef_ref[7, :, lre]
            hr = xr + (pr * cr - pi * ci)
            hi = xi + (pr * ci + pi * cr)
            st_ref[rows, lre] = hr
            st_ref[rows, lim] = hi
            return hr[SUBLANES - 1:, :], hi[SUBLANES - 1:, :]

        cr, ci = lax.fori_loop(0, tt // SUBLANES, group, (carry_ref[:, lre], carry_ref[:, lim]))
        carry_ref[:, lre] = cr
        carry_ref[:, lim] = ci

    o_ref[...] = _ssm_readout(st_ref[:, :gp], st_ref[:, gp:], u, gs_ref[...],
                              cre_ref, cim_ref, d_ref, wglu_ref, bglu_ref)
    hre_ref[...] = carry_ref[:, :gp]
    him_ref[...] = carry_ref[:, gp:]


def _const(shape):
    nd = len(shape)
    return pl.BlockSpec(shape, lambda *_: (0,) * nd)


def _pssm(u, gs, wb, coef, cre, cim, dsk, wglu, bglu):
    b, t, ssw = u.shape
    gp = cre.shape[0]
    tt = min(SSM_ROWS, t)
    row = pl.BlockSpec((None, tt, ssw), lambda i, j: (i, j, 0))
    st = pl.BlockSpec((None, 1, gp), lambda i, j: (i, 0, 0))
    return pl.pallas_call(
        _pssm_kernel,
        grid=(b, t // tt),
        in_specs=[row, row, _const(wb.shape), _const(coef.shape), _const(cre.shape), _const(cim.shape),
                  _const(dsk.shape), _const(wglu.shape), _const(bglu.shape)],
        out_specs=[row, st, st],
        out_shape=[jax.ShapeDtypeStruct((b, t, ssw), F32),
                   jax.ShapeDtypeStruct((b, 1, gp), F32),
                   jax.ShapeDtypeStruct((b, 1, gp), F32)],
        scratch_shapes=[pltpu.VMEM((tt, 2 * gp), F32), pltpu.VMEM((1, 2 * gp), F32)],
        compiler_params=_params("arbitrary", "arbitrary"),
        name="pssm",
    )(u, gs, wb, coef, cre, cim, dsk, wglu, bglu)


def _sssm_kernel(u_ref, gs_ref, h0re_ref, h0im_ref, wb_ref, coef_ref, cre_ref, cim_ref, d_ref,
                 wglu_ref, bglu_ref, o_ref, hre_ref, him_ref):
    gp = cre_ref.shape[0]
    t = pl.program_id(0)

    @pl.when(t == 0)
    def _():
        hre_ref[...] = h0re_ref[...]
        him_ref[...] = h0im_ref[...]

    u = u_ref[...]
    bu = _dot(u.astype(BF16), wb_ref[...])
    ar = coef_ref[6, 0:1, :]
    ai = coef_ref[7, 0:1, :]
    hr, hi = hre_ref[...], him_ref[...]
    nr = ar * hr - ai * hi + bu[:, :gp]
    ni = ar * hi + ai * hr + bu[:, gp:]
    hre_ref[...] = nr
    him_ref[...] = ni
    o_ref[...] = _ssm_readout(nr, ni, u, gs_ref[...], cre_ref, cim_ref, d_ref, wglu_ref, bglu_ref)


def _sssm(u2, gs2, h0re, h0im, wb, coef, cre, cim, dsk, wglu, bglu, ts):
    nb = u2.shape[0]
    ssw = u2.shape[1] // ts
    gp = cre.shape[0]
    row = pl.BlockSpec((nb, ssw), lambda t: (0, t))
    return pl.pallas_call(
        _sssm_kernel,
        grid=(ts,),
        in_specs=[row, row, _const(h0re.shape), _const(h0im.shape), _const(wb.shape), _const(coef.shape),
                  _const(cre.shape), _const(cim.shape), _const(dsk.shape), _const(wglu.shape),
                  _const(bglu.shape)],
        out_specs=[row, _const((nb, gp)), _const((nb, gp))],
        out_shape=[jax.ShapeDtypeStruct((nb, ts * ssw), F32),
                   jax.ShapeDtypeStruct((nb, gp), F32),
                   jax.ShapeDtypeStruct((nb, gp), F32)],
        compiler_params=_params("arbitrary"),
        name="sssm",
    )(u2, gs2, h0re, h0im, wb, coef, cre, cim, dsk, wglu, bglu)


def _sattn_kernel(pt_ref, q_ref, kn_ref, vn_ref, bias_ref, *refs, npg):
    kp, vp = refs[:npg], refs[npg:2 * npg]
    o_ref, qbd_ref, pad_ref, acc_ref, c_ref = refs[2 * npg:]
    del pt_ref
    ts, sbw = q_ref.shape
    heads, dh, page = kp[0].shape
    rows = ts * heads
    step = pl.program_id(1)
    nbias = -bias_ref[...]
    incl_mat = (lax.broadcasted_iota(jnp.int32, (page, page), 0)
                >= lax.broadcasted_iota(jnp.int32, (page, page), 1)).astype(F32)
    head_match = (lax.broadcasted_iota(jnp.int32, (heads, sbw), 0)
                  == lax.broadcasted_iota(jnp.int32, (heads, sbw), 1) // dh)

    def suffix(zn):
        lk = jnp.minimum(zn, 0.0) - jnp.log(1.0 + jnp.exp(-jnp.abs(zn)))
        hi = lk.astype(BF16).astype(F32)
        incl = _dot(hi, incl_mat) + _dot(lk - hi, incl_mat)
        return incl - zn, incl[:, :1]

    @pl.when(step == 0)
    def _():
        q = q_ref[...]
        for t in range(ts):
            qrow = jnp.broadcast_to(-q[t:t + 1, :], (heads, sbw))
            qbd_ref[t * heads:(t + 1) * heads, :] = jnp.where(head_match, qrow, 0.0)
        pad_ref[...] = jnp.zeros_like(pad_ref)
        pad_ref[0:ts, :] = kn_ref[...]
        zt = _dot_nt(qbd_ref[...], pad_ref[...]) + nbias[:rows]
        key = lax.broadcasted_iota(jnp.int32, (rows, page), 1)
        qidx = lax.broadcasted_iota(jnp.int32, (rows, page), 0) // heads
        e, tot = suffix(jnp.where(key < qidx, zt, MASKED))
        pad_ref[0:ts, :] = vn_ref[...]
        acc_ref[...] = _dot(jnp.exp(e), pad_ref[...])
        c_ref[...] = tot

    qbd = qbd_ref[...]
    zn = jnp.concatenate([_dot(qbd, kp[j][...].reshape(sbw, page)) for j in range(npg)], axis=0) + nbias
    e, tot = suffix(zn)
    run = c_ref[...]
    offs = []
    for j in range(npg):
        offs.append(run)
        run = run + tot[j * rows:(j + 1) * rows]
    c_ref[...] = run
    a = jnp.exp(e + jnp.concatenate(offs, axis=0))
    acc = acc_ref[...]
    for j in range(npg):
        acc = acc + _dot_nt(a[j * rows:(j + 1) * rows], vp[j][...].reshape(sbw, page))
    acc_ref[...] = acc

    @pl.when(step == pl.num_programs(1) - 1)
    def _():
        for t in range(ts):
            blk = jnp.where(head_match, acc[t * heads:(t + 1) * heads, :], 0.0)
            o_ref[t:t + 1, :] = jnp.sum(blk, axis=0, keepdims=True)


def _sattn(page_table, q3, kn3, vn3, bias_col, ckt, cvt):
    nb, ts, sbw = q3.shape
    npages = page_table.shape[1]
    heads, dh, page = ckt.shape[2:]
    npg = min(SAMPLE_PAGES, npages)
    rows = ts * heads
    tok = pl.BlockSpec((None, ts, sbw), lambda b, c, pt: (b, 0, 0))

    def page_spec(j):
        return pl.BlockSpec((None, None, heads, dh, page),
                            lambda b, c, pt, j=j: (0, pt[b, npages - 1 - npg * c - j], 0, 0, 0))

    pages = [page_spec(j) for j in range(npg)]
    grid_spec = pltpu.PrefetchScalarGridSpec(
        num_scalar_prefetch=1,
        grid=(nb, npages // npg),
        in_specs=[tok, tok, tok, pl.BlockSpec(bias_col.shape, lambda b, c, pt: (0, 0))] + pages + pages,
        out_specs=tok,
        scratch_shapes=[pltpu.VMEM((rows, sbw), F32), pltpu.VMEM((page, sbw), F32),
                        pltpu.VMEM((rows, sbw), F32), pltpu.VMEM((rows, 1), F32)],
    )
    return pl.pallas_call(
        functools.partial(_sattn_kernel, npg=npg),
        grid_spec=grid_spec,
        out_shape=jax.ShapeDtypeStruct((nb, ts, sbw), F32),
        compiler_params=_params("arbitrary", "arbitrary"),
        name="sattn",
    )(page_table, q3, kn3, vn3, bias_col, *([ckt] * npg), *([cvt] * npg))


def _outproj_kernel(x_ref, gate_ref, att_ref, gsb_ref, ssm_ref, w_ref, fg_ref, o_ref):
    sbw = att_ref.shape[-1]
    mixed = (_dot((att_ref[...] * gsb_ref[...]).astype(BF16), w_ref[:sbw, :])
             + _dot(ssm_ref[...].astype(BF16), w_ref[sbw:, :]))
    y = x_ref[...] + gate_ref[...] * mixed
    ms = jnp.mean(y * y, axis=-1, keepdims=True)
    o_ref[...] = y * lax.rsqrt(ms + NORM_EPS) * fg_ref[...]


def _outproj_prompt(x, m3, att, gsb, ssm, w_out, fg):
    b, t, d = x.shape
    sbw, ssw = att.shape[-1], ssm.shape[-1]
    tm = min(PROMPT_ROWS, t)

    def row(w):
        return pl.BlockSpec((None, tm, w), lambda i, j: (i, j, 0))

    return pl.pallas_call(
        _outproj_kernel,
        grid=(b, t // tm),
        in_specs=[row(d), pl.BlockSpec((None, 1, d), lambda i, j: (i, 0, 2)),
                  row(sbw), row(sbw), row(ssw), _const(w_out.shape), _const(fg.shape)],
        out_specs=row(d),
        out_shape=jax.ShapeDtypeStruct((b, t, d), F32),
        compiler_params=_params("arbitrary", "arbitrary"),
        name="outproj_prompt",
    )(x, m3, att, gsb, ssm, w_out, fg)


def _outproj_sample(x2, m, att2, gsb2, ssm2, w_out, fg, ts):
    nb = x2.shape[0]
    d = x2.shape[1] // ts
    sbw, ssw = att2.shape[1] // ts, ssm2.shape[1] // ts

    def col(w):
        return pl.BlockSpec((nb, w), lambda t: (0, t))

    return pl.pallas_call(
        _outproj_kernel,
        grid=(ts,),
        in_specs=[col(d), pl.BlockSpec((nb, d), lambda t: (0, 2)),
                  col(sbw), col(sbw), col(ssw), _const(w_out.shape), _const(fg.shape)],
        out_specs=col(d),
        out_shape=jax.ShapeDtypeStruct((nb, ts * d), F32),
        compiler_params=_params("arbitrary"),
        name="outproj_sample",
    )(x2, m, att2, gsb2, ssm2, w_out, fg)


def _block_diag(w):
    g, a, b = w.shape
    eye = jnp.eye(g, dtype=w.dtype)
    return (eye[:, None, :, None] * w[:, :, None, :]).reshape(g * a, g * b)


def kernel(x_prompt, x_sample, c_prompt, c_sample, cache_k, cache_v, page_table, state_ssm_re, state_ssm_im, norm_g, w_ada, b_ada, w_in, sb_bias, ssm_a_re, ssm_a_im, ssm_log_dt, ssm_b_re, ssm_b_im, ssm_c_re, ssm_c_im, ssm_d, w_glu, b_glu, w_out, final_norm_g):
    depth = w_in.shape[0]
    assert depth == 1, "single mixer layer"
    b, t, d = x_prompt.shape
    nb, ts, _ = x_sample.shape
    heads = sb_bias.shape[1]
    dh = cache_k.shape[-1]
    sbw = heads * dh
    n_pool, page = cache_k.shape[1], cache_k.shape[2]
    g, p = ssm_a_re.shape[1:]
    ch = ssm_b_re.shape[-1]
    gp, ssw = g * p, g * ch
    qscale = float(dh) ** -0.5

    w_in_b = w_in[0].astype(BF16)
    w_out_b = w_out[0].astype(BF16)
    w_glu_b = w_glu[0].astype(BF16)
    ng = norm_g[0].reshape(1, d)
    fg = final_norm_g.reshape(1, d)
    bglu = b_glu[0].reshape(1, -1)
    dsk = ssm_d[0].reshape(1, ssw)
    are = ssm_a_re[0].reshape(1, gp)
    aim = ssm_a_im[0].reshape(1, gp)
    ldt = jnp.repeat(ssm_log_dt[0], p).reshape(1, gp)
    bre_bd = _block_diag(jnp.swapaxes(ssm_b_re[0], 1, 2))
    bim_bd = _block_diag(jnp.swapaxes(ssm_b_im[0], 1, 2))
    cre_bd = _block_diag(jnp.swapaxes(ssm_c_re[0], 1, 2)).astype(BF16)
    cim_bd = _block_diag(jnp.swapaxes(ssm_c_im[0], 1, 2)).astype(BF16)

    m = _ada(jnp.concatenate([c_prompt, c_sample], axis=0), w_ada[0], b_ada[0].reshape(1, -1))
    m_p = m[:b].reshape(b, 1, 3 * d)
    m_s = m[b:]
    coef, wb = _ssm_param(are, aim, ldt, bre_bd, bim_bd)

    q, k_p, v_p, gsb, u, gssm = _inproj_prompt(x_prompt, ng, m_p, w_in_b, sbw, qscale * LOG2E)
    att = _pattn(sb_bias[0], q, k_p, v_p, dh)
    ssm, hre_p, him_p = _pssm(u, gssm, wb, coef, cre_bd, cim_bd, dsk, w_glu_b, bglu)
    y_p = _outproj_prompt(x_prompt, m_p, att, gsb, ssm, w_out_b, fg)

    x2 = x_sample.reshape(nb, ts * d)
    q2, k2, v2, gsb2, u2, gssm2 = _inproj_sample(x2, ng, m_s, w_in_b, sbw, qscale, ts)
    bias_col = jnp.tile(sb_bias[0], ts * min(SAMPLE_PAGES, page_table.shape[1])).reshape(-1, 1)
    att_s = _sattn(page_table, q2.reshape(nb, ts, sbw), k2.reshape(nb, ts, sbw), v2.reshape(nb, ts, sbw),
                   bias_col, jnp.transpose(cache_k, (0, 1, 3, 4, 2)), jnp.transpose(cache_v, (0, 1, 3, 4, 2)))
    ssm2, hre_s, him_s = _sssm(u2, gssm2, state_ssm_re[0].reshape(nb, gp), state_ssm_im[0].reshape(nb, gp),
                               wb, coef, cre_bd, cim_bd, dsk, w_glu_b, bglu, ts)
    y_s = _outproj_sample(x2, m_s, att_s.reshape(nb, ts * sbw), gsb2, ssm2, w_out_b, fg, ts)

    return (y_p, y_s.reshape(nb, ts, d),
            k_p.reshape(1, b, t, heads, dh), v_p.reshape(1, b, t, heads, dh),
            hre_p.reshape(1, b, g, p), him_p.reshape(1, b, g, p),
            k2.reshape(1, nb, ts, heads, dh), v2.reshape(1, nb, ts, heads, dh),
            hre_s.reshape(1, nb, g, p), him_s.reshape(1, nb, g, p))
```

```python
import functools

import jax
import jax.numpy as jnp
from jax import lax
from jax.experimental import pallas as pl
from jax.experimental.pallas import tpu as pltpu

F32 = jnp.float32
BF16 = jnp.bfloat16
NORM_EPS = 1e-6
SUBLANES = 8
LANES = 128
VMEM_LIMIT = 48 * 1024 * 1024

PROMPT_ROWS = 512
ATTN_BLOCK = 256
SSM_ROWS = 256
SCAN_LANES = 512
SAMPLE_PAGES = 8
MASKED = 1e30
LOG2E = 1.4426950408889634


def _params(*sem):
    return pltpu.CompilerParams(dimension_semantics=sem, vmem_limit_bytes=VMEM_LIMIT)


def _silu(x):
    return x * jax.nn.sigmoid(x)


def _dot(a, b):
    return jnp.dot(a, b, preferred_element_type=F32)


def _dot_nt(a, b):
    return lax.dot_general(a, b, (((1,), (1,)), ((), ())), preferred_element_type=F32)


def _ada_kernel(c_ref, w_ref, b_ref, o_ref):
    c = c_ref[...]
    o_ref[...] = _dot(_silu(c).astype(BF16), w_ref[...].astype(BF16)) + b_ref[...]


def _ada(c, w, b):
    n, d = c.shape
    n3 = w.shape[1]
    return pl.pallas_call(
        _ada_kernel,
        grid=(n3 // d,),
        in_specs=[pl.BlockSpec((n, d), lambda j: (0, 0)),
                  pl.BlockSpec((d, d), lambda j: (0, j)),
                  pl.BlockSpec((1, d), lambda j: (0, j))],
        out_specs=pl.BlockSpec((n, d), lambda j: (0, j)),
        out_shape=jax.ShapeDtypeStruct((n, n3), F32),
        compiler_params=_params("arbitrary"),
        name="ada",
    )(c, w, b)


def _cmul(a, b):
    return a[0] * b[0] - a[1] * b[1], a[0] * b[1] + a[1] * b[0]


def _ssm_param_kernel(are_ref, aim_ref, ldt_ref, bre_ref, bim_ref, coef_ref, wb_ref):
    gp = are_ref.shape[1]
    are, aim = are_ref[...], aim_ref[...]
    dt = jnp.exp(ldt_ref[...])
    mag = jnp.exp(dt * are)
    ar = mag * jnp.cos(dt * aim)
    ai = mag * jnp.sin(dt * aim)
    den = are * are + aim * aim
    nre = ar - 1.0
    fre = (nre * are + ai * aim) / den
    fim = (ai * are - nre * aim) / den
    bre, bim = bre_ref[...], bim_ref[...]
    wb_ref[:, :gp] = (fre * bre - fim * bim).astype(BF16)
    wb_ref[:, gp:] = (fre * bim + fim * bre).astype(BF16)

    pw = [(ar, ai)]
    for _ in range(SUBLANES - 1):
        pw.append(_cmul(pw[-1], (ar, ai)))
    row = lax.broadcasted_iota(jnp.int32, (SUBLANES, gp), 0)
    for n, d in enumerate((1, 2, 4)):
        for part in range(2):
            coef_ref[2 * n + part] = jnp.where(row >= d, pw[d - 1][part], 0.0)
    for part in range(2):
        acc = jnp.zeros((SUBLANES, gp), F32)
        for r in range(SUBLANES):
            acc = jnp.where(row == r, pw[r][part], acc)
        coef_ref[6 + part] = acc


def _ssm_param(are, aim, ldt, bre_bd, bim_bd):
    gc, gp = bre_bd.shape
    return pl.pallas_call(
        _ssm_param_kernel,
        out_shape=(jax.ShapeDtypeStruct((8, SUBLANES, gp), F32),
                   jax.ShapeDtypeStruct((gc, 2 * gp), BF16)),
        compiler_params=_params(),
        name="ssm_param",
    )(are, aim, ldt, bre_bd, bim_bd)


def _inproj_kernel(x_ref, ng_ref, shift_ref, scale_ref, w_ref,
                   q_ref, k_ref, v_ref, gsb_ref, u_ref, gssm_ref, *, qscale):
    x = x_ref[...]
    ms = jnp.mean(x * x, axis=-1, keepdims=True)
    h = x * lax.rsqrt(ms + NORM_EPS) * ng_ref[...]
    h = (h * (1.0 + scale_ref[...]) + shift_ref[...]).astype(BF16)
    w = q_ref.shape[-1]

    def proj(c):
        return _dot(h, w_ref[:, c * w:(c + 1) * w])

    q_ref[...] = proj(0) * qscale
    k_ref[...] = proj(1)
    v_ref[...] = proj(2)
    gsb_ref[...] = _silu(proj(3))
    u_ref[...] = proj(4)
    gssm_ref[...] = _silu(proj(5))


def _inproj_prompt(x, ng, m3, w_in, sbw, qscale):
    b, t, d = x.shape
    tm = min(PROMPT_ROWS, t)
    row = pl.BlockSpec((None, tm, d), lambda i, j: (i, j, 0))
    out = pl.BlockSpec((None, tm, sbw), lambda i, j: (i, j, 0))
    return pl.pallas_call(
        functools.partial(_inproj_kernel, qscale=qscale),
        grid=(b, t // tm),
        in_specs=[row,
                  pl.BlockSpec((1, d), lambda i, j: (0, 0)),
                  pl.BlockSpec((None, 1, d), lambda i, j: (i, 0, 0)),
                  pl.BlockSpec((None, 1, d), lambda i, j: (i, 0, 1)),
                  pl.BlockSpec(w_in.shape, lambda i, j: (0, 0))],
        out_specs=[out] * 6,
        out_shape=[jax.ShapeDtypeStruct((b, t, sbw), F32)] * 6,
        compiler_params=_params("arbitrary", "arbitrary"),
        name="inproj_prompt",
    )(x, ng, m3, m3, w_in)


def _inproj_sample(x2, ng, m, w_in, sbw, qscale, ts):
    nb = x2.shape[0]
    d = x2.shape[1] // ts
    out = pl.BlockSpec((nb, sbw), lambda t: (0, t))
    return pl.pallas_call(
        functools.partial(_inproj_kernel, qscale=qscale),
        grid=(ts,),
        in_specs=[pl.BlockSpec((nb, d), lambda t: (0, t)),
                  pl.BlockSpec((1, d), lambda t: (0, 0)),
                  pl.BlockSpec((nb, d), lambda t: (0, 0)),
                  pl.BlockSpec((nb, d), lambda t: (0, 1)),
                  pl.BlockSpec(w_in.shape, lambda t: (0, 0))],
        out_specs=[out] * 6,
        out_shape=[jax.ShapeDtypeStruct((nb, ts * sbw), F32)] * 6,
        compiler_params=_params("arbitrary"),
        name="inproj_sample",
    )(x2, ng, m, m, w_in)


def _pattn_kernel(bias_ref, q_ref, k_ref, v_ref, o_ref, z_ref, e_ref, tot_ref, c_ref, *, dh):
    tq = q_ref.shape[0]
    hp = pl.program_id(1)
    i = pl.program_id(2)
    ntiles = i + 1
    first = lax.broadcasted_iota(jnp.int32, (1, 2 * dh), 1) < dh
    q = q_ref[...]
    nq = (jnp.where(first, -q, 0.0), jnp.where(first, 0.0, -q))
    nbias = (-LOG2E * bias_ref[2 * hp], -LOG2E * bias_ref[2 * hp + 1])
    r = lax.broadcasted_iota(jnp.int32, (tq, tq), 0)
    c = lax.broadcasted_iota(jnp.int32, (tq, tq), 1)
    incl_mat = (r >= c).astype(F32)

    def rows(n):
        return pl.ds(pl.multiple_of((i - n) * tq, tq), tq)

    def stage1(n, masked=False):
        kb = k_ref[rows(n), :]
        for h in range(2):
            zn = _dot_nt(nq[h], kb) + nbias[h]
            if masked:
                zn = jnp.where(c < r, zn, MASKED)
            z_ref[h] = zn

    def stage2():
        for h in range(2):
            zn = z_ref[h]
            lk = jnp.minimum(zn, 0.0) - LOG2E * jnp.log(1.0 + jnp.exp2(-jnp.abs(zn)))
            incl = _dot(lk, incl_mat)
            e_ref[h] = incl - zn
            tot_ref[h] = incl[:, :tot_ref.shape[-1]]

    def stage3(n):
        vb = v_ref[rows(n), :]
        outs = []
        for h in range(2):
            cc = c_ref[h]
            outs.append(_dot(jnp.exp2(e_ref[h] + jnp.concatenate([cc] * (tq // cc.shape[1]), axis=1)), vb))
            c_ref[h] = cc + tot_ref[h][:, :1]
        o_ref[...] += jnp.where(first, outs[0], outs[1])

    o_ref[...] = jnp.zeros_like(o_ref)
    c_ref[...] = jnp.zeros_like(c_ref)
    stage1(0, masked=True)

    @pl.when(ntiles == 1)
    def _():
        stage2()

    @pl.when(ntiles >= 2)
    def _():
        stage2()
        stage1(1)

    def steady(n, carry):
        stage3(n - 2)
        stage2()
        stage1(n)
        return carry

    lax.fori_loop(2, ntiles, steady, 0)

    @pl.when(ntiles >= 2)
    def _():
        stage3(ntiles - 2)
        stage2()

    stage3(ntiles - 1)


def _pattn(bias, q, k, v, dh):
    b, t, sbw = q.shape
    tq = min(ATTN_BLOCK, t)
    blk = pl.BlockSpec((None, tq, 2 * dh), lambda bi, hp, i: (bi, i, hp))
    full = pl.BlockSpec((None, t, 2 * dh), lambda bi, hp, i: (bi, 0, hp))
    return pl.pallas_call(
        functools.partial(_pattn_kernel, dh=dh),
        grid=(b, sbw // (2 * dh), t // tq),
        in_specs=[pl.BlockSpec(memory_space=pltpu.SMEM), blk, full, full],
        out_specs=blk,
        out_shape=jax.ShapeDtypeStruct((b, t, sbw), F32),
        scratch_shapes=[pltpu.VMEM((2, tq, tq), F32), pltpu.VMEM((2, tq, tq), F32),
                        pltpu.VMEM((2, tq, 2 * dh), F32), pltpu.VMEM((2, tq, 2 * dh), F32)],
        compiler_params=_params("arbitrary", "arbitrary", "arbitrary"),
        name="pattn",
    )(bias, q, k, v)


def _ssm_drive(u, wb_ref):
    gc, gp = wb_ref.shape[0], wb_ref.shape[1] // 2
    cols = LANES * gp // gc
    ub = u.astype(BF16)
    pieces = []
    for part in range(2):
        for m in range(gc // LANES):
            rows = slice(m * LANES, (m + 1) * LANES)
            pieces.append(_dot(ub[:, rows], wb_ref[rows, part * gp + m * cols:part * gp + (m + 1) * cols]))
    return pieces


def _ssm_readout(hre, him, u, gs, cre_ref, cim_ref, d_ref, wglu_ref, bglu_ref):
    gp, gc = cre_ref.shape
    cols = LANES * gp // gc
    hre, him = hre.astype(BF16), him.astype(BF16)
    y = []
    for m in range(gc // LANES):
        st, ch = slice(m * cols, (m + 1) * cols), slice(m * LANES, (m + 1) * LANES)
        y.append(_dot(hre[:, st], cre_ref[st, ch]) - _dot(him[:, st], cim_ref[st, ch]))
    y = jnp.concatenate(y, axis=1) + d_ref[...] * u
    y = jax.nn.gelu(y)
    z = _dot(y.astype(BF16), wglu_ref[...]) + bglu_ref[...]
    w = z.shape[-1] // 2
    return z[:, :w] * jax.nn.sigmoid(z[:, w:]) * gs


def _pssm_kernel(u_ref, gs_ref, wb_ref, coef_ref, cre_ref, cim_ref, d_ref, wglu_ref, bglu_ref,
                 o_ref, hre_ref, him_ref, st_ref, carry_ref):
    tt = u_ref.shape[0]
    gp = cre_ref.shape[0]
    j = pl.program_id(1)

    @pl.when(j == 0)
    def _():
        carry_ref[...] = jnp.zeros_like(carry_ref)

    u = u_ref[...]
    pieces = _ssm_drive(u, wb_ref)
    cols = pieces[0].shape[1]
    for n, piece in enumerate(pieces):
        st_ref[:, n * cols:(n + 1) * cols] = piece

    for lc in range(gp // SCAN_LANES):
        lre = pl.ds(lc * SCAN_LANES, SCAN_LANES)
        lim = pl.ds(gp + lc * SCAN_LANES, SCAN_LANES)

        def group(g, carry, lre=lre, lim=lim):
            cr, ci = carry
            rows = pl.ds(pl.multiple_of(g * SUBLANES, SUBLANES), SUBLANES)
            xr = st_ref[rows, lre]
            xi = st_ref[rows, lim]
            for n, d in enumerate((1, 2, 4)):
                ar = coef_ref[2 * n, :, lre]
                ai = coef_ref[2 * n + 1, :, lre]
                sr = pltpu.roll(xr, d, 0)
                si = pltpu.roll(xi, d, 0)
                xr, xi = xr + (ar * sr - ai * si), xi + (ar * si + ai * sr)
            pr = coef_ref[6, :, lre]
            pi = coef_ref[7, :, lre]
            hr = xr + (pr * cr - pi * ci)
            hi = xi + (pr * ci + pi * cr)
            st_ref[rows, lre] = hr
            st_ref[rows, lim] = hi
            return hr[SUBLANES - 1:, :], hi[SUBLANES - 1:, :]

        cr, ci = lax.fori_loop(0, tt // SUBLANES, group, (carry_ref[:, lre], carry_ref[:, lim]))
        carry_ref[:, lre] = cr
        carry_ref[:, lim] = ci

    o_ref[...] = _ssm_readout(st_ref[:, :gp], st_ref[:, gp:], u, gs_ref[...],
                              cre_ref, cim_ref, d_ref, wglu_ref, bglu_ref)
    hre_ref[...] = carry_ref[:, :gp]
    him_ref[...] = carry_ref[:, gp:]


def _const(shape):
    nd = len(shape)
    return pl.BlockSpec(shape, lambda *_: (0,) * nd)


def _pssm(u, gs, wb, coef, cre, cim, dsk, wglu, bglu):
    b, t, ssw = u.shape
    gp = cre.shape[0]
    tt = min(SSM_ROWS, t)
    row = pl.BlockSpec((None, tt, ssw), lambda i, j: (i, j, 0))
    st = pl.BlockSpec((None, 1, gp), lambda i, j: (i, 0, 0))
    return pl.pallas_call(
        _pssm_kernel,
        grid=(b, t // tt),
        in_specs=[row, row, _const(wb.shape), _const(coef.shape), _const(cre.shape), _const(cim.shape),
                  _const(dsk.shape), _const(wglu.shape), _const(bglu.shape)],
        out_specs=[row, st, st],
        out_shape=[jax.ShapeDtypeStruct((b, t, ssw), F32),
                   jax.ShapeDtypeStruct((b, 1, gp), F32),
                   jax.ShapeDtypeStruct((b, 1, gp), F32)],
        scratch_shapes=[pltpu.VMEM((tt, 2 * gp), F32), pltpu.VMEM((1, 2 * gp), F32)],
        compiler_params=_params("arbitrary", "arbitrary"),
        name="pssm",
    )(u, gs, wb, coef, cre, cim, dsk, wglu, bglu)


def _sssm_kernel(u_ref, gs_ref, h0re_ref, h0im_ref, wb_ref, coef_ref, cre_ref, cim_ref, d_ref,
                 wglu_ref, bglu_ref, o_ref, hre_ref, him_ref):
    gp = cre_ref.shape[0]
    t = pl.program_id(0)

    @pl.when(t == 0)
    def _():
        hre_ref[...] = h0re_ref[...]
        him_ref[...] = h0im_ref[...]

    u = u_ref[...]
    bu = jnp.concatenate(_ssm_drive(u, wb_ref), axis=1)
    ar = coef_ref[6, 0:1, :]
    ai = coef_ref[7, 0:1, :]
    hr, hi = hre_ref[...], him_ref[...]
    nr = ar * hr - ai * hi + bu[:, :gp]
    ni = ar * hi + ai * hr + bu[:, gp:]
    hre_ref[...] = nr
    him_ref[...] = ni
    o_ref[...] = _ssm_readout(nr, ni, u, gs_ref[...], cre_ref, cim_ref, d_ref, wglu_ref, bglu_ref)


def _sssm(u2, gs2, h0re, h0im, wb, coef, cre, cim, dsk, wglu, bglu, ts):
    nb = u2.shape[0]
    ssw = u2.shape[1] // ts
    gp = cre.shape[0]
    row = pl.BlockSpec((nb, ssw), lambda t: (0, t))
    return pl.pallas_call(
        _sssm_kernel,
        grid=(ts,),
        in_specs=[row, row, _const(h0re.shape), _const(h0im.shape), _const(wb.shape), _const(coef.shape),
                  _const(cre.shape), _const(cim.shape), _const(dsk.shape), _const(wglu.shape),
                  _const(bglu.shape)],
        out_specs=[row, _const((nb, gp)), _const((nb, gp))],
        out_shape=[jax.ShapeDtypeStruct((nb, ts * ssw), F32),
                   jax.ShapeDtypeStruct((nb, gp), F32),
                   jax.ShapeDtypeStruct((nb, gp), F32)],
        compiler_params=_params("arbitrary"),
        name="sssm",
    )(u2, gs2, h0re, h0im, wb, coef, cre, cim, dsk, wglu, bglu)


def _sattn_kernel(pt_ref, q_ref, kn_ref, vn_ref, bias_ref, *refs, npg):
    kp, vp = refs[:npg], refs[npg:2 * npg]
    o_ref, qbd_ref, pad_ref, acc_ref, c_ref = refs[2 * npg:]
    del pt_ref
    ts, sbw = q_ref.shape
    heads, dh, page = kp[0].shape
    rows = ts * heads
    step = pl.program_id(1)
    nbias = -bias_ref[...]
    incl_mat = (lax.broadcasted_iota(jnp.int32, (page, page), 0)
                >= lax.broadcasted_iota(jnp.int32, (page, page), 1)).astype(F32)
    head_match = (lax.broadcasted_iota(jnp.int32, (heads, sbw), 0)
                  == lax.broadcasted_iota(jnp.int32, (heads, sbw), 1) // dh)

    def suffix(zn):
        lk = jnp.minimum(zn, 0.0) - jnp.log(1.0 + jnp.exp(-jnp.abs(zn)))
        incl = _dot(lk, incl_mat)
        return incl - zn, incl[:, :1]

    @pl.when(step == 0)
    def _():
        q = q_ref[...]
        for t in range(ts):
            qrow = jnp.broadcast_to(-q[t:t + 1, :], (heads, sbw))
            qbd_ref[t * heads:(t + 1) * heads, :] = jnp.where(head_match, qrow, 0.0)
        pad_ref[...] = jnp.zeros_like(pad_ref)
        pad_ref[0:ts, :] = kn_ref[...]
        zt = _dot_nt(qbd_ref[...], pad_ref[...]) + nbias[:rows]
        key = lax.broadcasted_iota(jnp.int32, (rows, page), 1)
        qidx = lax.broadcasted_iota(jnp.int32, (rows, page), 0) // heads
        e, tot = suffix(jnp.where(key < qidx, zt, MASKED))
        pad_ref[0:ts, :] = vn_ref[...]
        acc_ref[...] = _dot(jnp.exp(e), pad_ref[...])
        c_ref[...] = tot

    qbd = qbd_ref[...]
    zn = jnp.concatenate([_dot(qbd, kp[j][...].reshape(sbw, page)) for j in range(npg)], axis=0) + nbias
    e, tot = suffix(zn)
    run = c_ref[...]
    offs = []
    for j in range(npg):
        offs.append(run)
        run = run + tot[j * rows:(j + 1) * rows]
    c_ref[...] = run
    a = jnp.exp(e + jnp.concatenate(offs, axis=0))
    acc = acc_ref[...]
    for j in range(npg):
        acc = acc + _dot_nt(a[j * rows:(j + 1) * rows], vp[j][...].reshape(sbw, page))
    acc_ref[...] = acc

    @pl.when(step == pl.num_programs(1) - 1)
    def _():
        for t in range(ts):
            blk = jnp.where(head_match, acc[t * heads:(t + 1) * heads, :], 0.0)
            o_ref[t:t + 1, :] = jnp.sum(blk, axis=0, keepdims=True)


def _sattn(page_table, q3, kn3, vn3, bias_col, ckt, cvt):
    nb, ts, sbw = q3.shape
    npages = page_table.shape[1]
    heads, dh, page = ckt.shape[2:]
    npg = min(SAMPLE_PAGES, npages)
    rows = ts * heads
    tok = pl.BlockSpec((None, ts, sbw), lambda b, c, pt: (b, 0, 0))

    def page_spec(j):
        return pl.BlockSpec((None, None, heads, dh, page),
                            lambda b, c, pt, j=j: (0, pt[b, npages - 1 - npg * c - j], 0, 0, 0))

    pages = [page_spec(j) for j in range(npg)]
    grid_spec = pltpu.PrefetchScalarGridSpec(
        num_scalar_prefetch=1,
        grid=(nb, npages // npg),
        in_specs=[tok, tok, tok, pl.BlockSpec(bias_col.shape, lambda b, c, pt: (0, 0))] + pages + pages,
        out_specs=tok,
        scratch_shapes=[pltpu.VMEM((rows, sbw), F32), pltpu.VMEM((page, sbw), F32),
                        pltpu.VMEM((rows, sbw), F32), pltpu.VMEM((rows, 1), F32)],
    )
    return pl.pallas_call(
        functools.partial(_sattn_kernel, npg=npg),
        grid_spec=grid_spec,
        out_shape=jax.ShapeDtypeStruct((nb, ts, sbw), F32),
        compiler_params=_params("arbitrary", "arbitrary"),
        name="sattn",
    )(page_table, q3, kn3, vn3, bias_col, *([ckt] * npg), *([cvt] * npg))


def _outproj_kernel(x_ref, gate_ref, att_ref, gsb_ref, ssm_ref, w_ref, fg_ref, o_ref):
    sbw = att_ref.shape[-1]
    mixed = (_dot((att_ref[...] * gsb_ref[...]).astype(BF16), w_ref[:sbw, :])
             + _dot(ssm_ref[...].astype(BF16), w_ref[sbw:, :]))
    y = x_ref[...] + gate_ref[...] * mixed
    ms = jnp.mean(y * y, axis=-1, keepdims=True)
    o_ref[...] = y * lax.rsqrt(ms + NORM_EPS) * fg_ref[...]


def _outproj_prompt(x, m3, att, gsb, ssm, w_out, fg):
    b, t, d = x.shape
    sbw, ssw = att.shape[-1], ssm.shape[-1]
    tm = min(PROMPT_ROWS, t)

    def row(w):
        return pl.BlockSpec((None, tm, w), lambda i, j: (i, j, 0))

    return pl.pallas_call(
        _outproj_kernel,
        grid=(b, t // tm),
        in_specs=[row(d), pl.BlockSpec((None, 1, d), lambda i, j: (i, 0, 2)),
                  row(sbw), row(sbw), row(ssw), _const(w_out.shape), _const(fg.shape)],
        out_specs=row(d),
        out_shape=jax.ShapeDtypeStruct((b, t, d), F32),
        compiler_params=_params("arbitrary", "arbitrary"),
        name="outproj_prompt",
    )(x, m3, att, gsb, ssm, w_out, fg)


def _outproj_sample(x2, m, att2, gsb2, ssm2, w_out, fg, ts):
    nb = x2.shape[0]
    d = x2.shape[1] // ts
    sbw, ssw = att2.shape[1] // ts, ssm2.shape[1] // ts

    def col(w):
        return pl.BlockSpec((nb, w), lambda t: (0, t))

    return pl.pallas_call(
        _outproj_kernel,
        grid=(ts,),
        in_specs=[col(d), pl.BlockSpec((nb, d), lambda t: (0, 2)),
                  col(sbw), col(sbw), col(ssw), _const(w_out.shape), _const(fg.shape)],
        out_specs=col(d),
        out_shape=jax.ShapeDtypeStruct((nb, ts * d), F32),
        compiler_params=_params("arbitrary"),
        name="outproj_sample",
    )(x2, m, att2, gsb2, ssm2, w_out, fg)


def _block_diag(w):
    g, a, b = w.shape
    eye = jnp.eye(g, dtype=w.dtype)
    return (eye[:, None, :, None] * w[:, :, None, :]).reshape(g * a, g * b)


def kernel(x_prompt, x_sample, c_prompt, c_sample, cache_k, cache_v, page_table, state_ssm_re, state_ssm_im, norm_g, w_ada, b_ada, w_in, sb_bias, ssm_a_re, ssm_a_im, ssm_log_dt, ssm_b_re, ssm_b_im, ssm_c_re, ssm_c_im, ssm_d, w_glu, b_glu, w_out, final_norm_g):
    depth = w_in.shape[0]
    assert depth == 1, "single mixer layer"
    b, t, d = x_prompt.shape
    nb, ts, _ = x_sample.shape
    heads = sb_bias.shape[1]
    dh = cache_k.shape[-1]
    sbw = heads * dh
    n_pool, page = cache_k.shape[1], cache_k.shape[2]
    g, p = ssm_a_re.shape[1:]
    ch = ssm_b_re.shape[-1]
    gp, ssw = g * p, g * ch
    qscale = float(dh) ** -0.5

    w_in_b = w_in[0].astype(BF16)
    w_out_b = w_out[0].astype(BF16)
    w_glu_b = w_glu[0].astype(BF16)
    ng = norm_g[0].reshape(1, d)
    fg = final_norm_g.reshape(1, d)
    bglu = b_glu[0].reshape(1, -1)
    dsk = ssm_d[0].reshape(1, ssw)
    are = ssm_a_re[0].reshape(1, gp)
    aim = ssm_a_im[0].reshape(1, gp)
    ldt = jnp.repeat(ssm_log_dt[0], p).reshape(1, gp)
    bre_bd = _block_diag(jnp.swapaxes(ssm_b_re[0], 1, 2))
    bim_bd = _block_diag(jnp.swapaxes(ssm_b_im[0], 1, 2))
    cre_bd = _block_diag(jnp.swapaxes(ssm_c_re[0], 1, 2)).astype(BF16)
    cim_bd = _block_diag(jnp.swapaxes(ssm_c_im[0], 1, 2)).astype(BF16)

    m = _ada(jnp.concatenate([c_prompt, c_sample], axis=0), w_ada[0], b_ada[0].reshape(1, -1))
    m_p = m[:b].reshape(b, 1, 3 * d)
    m_s = m[b:]
    coef, wb = _ssm_param(are, aim, ldt, bre_bd, bim_bd)

    q, k_p, v_p, gsb, u, gssm = _inproj_prompt(x_prompt, ng, m_p, w_in_b, sbw, qscale * LOG2E)
    att = _pattn(sb_bias[0], q, k_p, v_p, dh)
    ssm, hre_p, him_p = _pssm(u, gssm, wb, coef, cre_bd, cim_bd, dsk, w_glu_b, bglu)
    y_p = _outproj_prompt(x_prompt, m_p, att, gsb, ssm, w_out_b, fg)

    x2 = x_sample.reshape(nb, ts * d)
    q2, k2, v2, gsb2, u2, gssm2 = _inproj_sample(x2, ng, m_s, w_in_b, sbw, qscale, ts)
    bias_col = jnp.tile(sb_bias[0], ts * min(SAMPLE_PAGES, page_table.shape[1])).reshape(-1, 1)
    att_s = _sattn(page_table, q2.reshape(nb, ts, sbw), k2.reshape(nb, ts, sbw), v2.reshape(nb, ts, sbw),
                   bias_col, jnp.transpose(cache_k, (0, 1, 3, 4, 2)), jnp.transpose(cache_v, (0, 1, 3, 4, 2)))
    ssm2, hre_s, him_s = _sssm(u2, gssm2, state_ssm_re[0].reshape(nb, gp), state_ssm_im[0].reshape(nb, gp),
                               wb, coef, cre_bd, cim_bd, dsk, w_glu_b, bglu, ts)
    y_s = _outproj_sample(x2, m_s, att_s.reshape(nb, ts * sbw), gsb2, ssm2, w_out_b, fg, ts)

    return (y_p, y_s.reshape(nb, ts, d),
            k_p.reshape(1, b, t, heads, dh), v_p.reshape(1, b, t, heads, dh),
            hre_p.reshape(1, b, g, p), him_p.reshape(1, b, g, p),
            k2.reshape(1, nb, ts, heads, dh), v2.reshape(1, nb, ts, heads, dh),
            hre_s.reshape(1, nb, g, p), him_s.reshape(1, nb, g, p))
```

```python
import functools

import jax
import jax.numpy as jnp
from jax import lax
from jax.experimental import pallas as pl
from jax.experimental.pallas import tpu as pltpu

F32 = jnp.float32
BF16 = jnp.bfloat16
NORM_EPS = 1e-6
SUBLANES = 8
LANES = 128
VMEM_LIMIT = 48 * 1024 * 1024

PROMPT_ROWS = 512
ATTN_BLOCK = 256
SSM_ROWS = 256
SCAN_LANES = 512
SAMPLE_PAGES = 16
MASKED = 1e30
LOG2E = 1.4426950408889634


def _params(*sem):
    return pltpu.CompilerParams(dimension_semantics=sem, vmem_limit_bytes=VMEM_LIMIT)


def _silu(x):
    return x * jax.nn.sigmoid(x)


def _dot(a, b):
    return jnp.dot(a, b, preferred_element_type=F32)


def _dot_nt(a, b):
    return lax.dot_general(a, b, (((1,), (1,)), ((), ())), preferred_element_type=F32)


def _ada_kernel(c_ref, w_ref, b_ref, o_ref):
    c = c_ref[...]
    o_ref[...] = _dot(_silu(c).astype(BF16), w_ref[...].astype(BF16)) + b_ref[...]


def _ada(c, w, b):
    n, d = c.shape
    n3 = w.shape[1]
    return pl.pallas_call(
        _ada_kernel,
        grid=(n3 // d,),
        in_specs=[pl.BlockSpec((n, d), lambda j: (0, 0)),
                  pl.BlockSpec((d, d), lambda j: (0, j)),
                  pl.BlockSpec((1, d), lambda j: (0, j))],
        out_specs=pl.BlockSpec((n, d), lambda j: (0, j)),
        out_shape=jax.ShapeDtypeStruct((n, n3), F32),
        compiler_params=_params("arbitrary"),
        name="ada",
    )(c, w, b)


def _cmul(a, b):
    return a[0] * b[0] - a[1] * b[1], a[0] * b[1] + a[1] * b[0]


def _ssm_param_kernel(are_ref, aim_ref, ldt_ref, bre_ref, bim_ref, coef_ref, wb_ref):
    gp = are_ref.shape[1]
    are, aim = are_ref[...], aim_ref[...]
    dt = jnp.exp(ldt_ref[...])
    mag = jnp.exp(dt * are)
    ar = mag * jnp.cos(dt * aim)
    ai = mag * jnp.sin(dt * aim)
    den = are * are + aim * aim
    nre = ar - 1.0
    fre = (nre * are + ai * aim) / den
    fim = (ai * are - nre * aim) / den
    bre, bim = bre_ref[...], bim_ref[...]
    wb_ref[:, :gp] = (fre * bre - fim * bim).astype(BF16)
    wb_ref[:, gp:] = (fre * bim + fim * bre).astype(BF16)

    pw = [(ar, ai)]
    for _ in range(SUBLANES - 1):
        pw.append(_cmul(pw[-1], (ar, ai)))
    row = lax.broadcasted_iota(jnp.int32, (SUBLANES, gp), 0)
    for n, d in enumerate((1, 2, 4)):
        for part in range(2):
            coef_ref[2 * n + part] = jnp.where(row >= d, pw[d - 1][part], 0.0)
    for part in range(2):
        acc = jnp.zeros((SUBLANES, gp), F32)
        for r in range(SUBLANES):
            acc = jnp.where(row == r, pw[r][part], acc)
        coef_ref[6 + part] = acc


def _ssm_param(are, aim, ldt, bre_bd, bim_bd):
    gc, gp = bre_bd.shape
    return pl.pallas_call(
        _ssm_param_kernel,
        out_shape=(jax.ShapeDtypeStruct((8, SUBLANES, gp), F32),
                   jax.ShapeDtypeStruct((gc, 2 * gp), BF16)),
        compiler_params=_params(),
        name="ssm_param",
    )(are, aim, ldt, bre_bd, bim_bd)


def _inproj_kernel(x_ref, ng_ref, shift_ref, scale_ref, w_ref,
                   q_ref, k_ref, v_ref, gsb_ref, u_ref, gssm_ref, *, qscale):
    x = x_ref[...]
    ms = jnp.mean(x * x, axis=-1, keepdims=True)
    h = x * lax.rsqrt(ms + NORM_EPS) * ng_ref[...]
    h = (h * (1.0 + scale_ref[...]) + shift_ref[...]).astype(BF16)
    w = q_ref.shape[-1]

    def proj(c):
        return _dot(h, w_ref[:, c * w:(c + 1) * w])

    q_ref[...] = proj(0) * qscale
    k_ref[...] = proj(1)
    v_ref[...] = proj(2)
    gsb_ref[...] = _silu(proj(3))
    u_ref[...] = proj(4)
    gssm_ref[...] = _silu(proj(5))


def _inproj_prompt(x, ng, m3, w_in, sbw, qscale):
    b, t, d = x.shape
    tm = min(PROMPT_ROWS, t)
    row = pl.BlockSpec((None, tm, d), lambda i, j: (i, j, 0))
    out = pl.BlockSpec((None, tm, sbw), lambda i, j: (i, j, 0))
    return pl.pallas_call(
        functools.partial(_inproj_kernel, qscale=qscale),
        grid=(b, t // tm),
        in_specs=[row,
                  pl.BlockSpec((1, d), lambda i, j: (0, 0)),
                  pl.BlockSpec((None, 1, d), lambda i, j: (i, 0, 0)),
                  pl.BlockSpec((None, 1, d), lambda i, j: (i, 0, 1)),
                  pl.BlockSpec(w_in.shape, lambda i, j: (0, 0))],
        out_specs=[out] * 6,
        out_shape=[jax.ShapeDtypeStruct((b, t, sbw), F32)] * 6,
        compiler_params=_params("arbitrary", "arbitrary"),
        name="inproj_prompt",
    )(x, ng, m3, m3, w_in)


def _inproj_sample(x2, ng, m, w_in, sbw, qscale, ts):
    nb = x2.shape[0]
    d = x2.shape[1] // ts
    out = pl.BlockSpec((nb, sbw), lambda t: (0, t))
    return pl.pallas_call(
        functools.partial(_inproj_kernel, qscale=qscale),
        grid=(ts,),
        in_specs=[pl.BlockSpec((nb, d), lambda t: (0, t)),
                  pl.BlockSpec((1, d), lambda t: (0, 0)),
                  pl.BlockSpec((nb, d), lambda t: (0, 0)),
                  pl.BlockSpec((nb, d), lambda t: (0, 1)),
                  pl.BlockSpec(w_in.shape, lambda t: (0, 0))],
        out_specs=[out] * 6,
        out_shape=[jax.ShapeDtypeStruct((nb, ts * sbw), F32)] * 6,
        compiler_params=_params("arbitrary"),
        name="inproj_sample",
    )(x2, ng, m, m, w_in)


def _pattn_kernel(bias_ref, q_ref, k_ref, v_ref, o_ref, z_ref, e_ref, tot_ref, c_ref, *, dh):
    tq = q_ref.shape[0]
    hp = pl.program_id(1)
    i = pl.program_id(2)
    ntiles = i + 1
    first = lax.broadcasted_iota(jnp.int32, (1, 2 * dh), 1) < dh
    q = q_ref[...]
    nq = (jnp.where(first, -q, 0.0), jnp.where(first, 0.0, -q))
    nbias = (-LOG2E * bias_ref[2 * hp], -LOG2E * bias_ref[2 * hp + 1])
    r = lax.broadcasted_iota(jnp.int32, (tq, tq), 0)
    c = lax.broadcasted_iota(jnp.int32, (tq, tq), 1)
    incl_mat = (r >= c).astype(F32)

    def rows(n):
        return pl.ds(pl.multiple_of((i - n) * tq, tq), tq)

    def stage1(n, masked=False):
        kb = k_ref[rows(n), :]
        for h in range(2):
            zn = _dot_nt(nq[h], kb) + nbias[h]
            if masked:
                zn = jnp.where(c < r, zn, MASKED)
            z_ref[h] = zn

    def stage2():
        for h in range(2):
            zn = z_ref[h]
            lk = jnp.minimum(zn, 0.0) - LOG2E * jnp.log(1.0 + jnp.exp2(-jnp.abs(zn)))
            incl = _dot(lk, incl_mat)
            e_ref[h] = incl - zn
            tot_ref[h] = incl[:, :tot_ref.shape[-1]]

    def stage3(n):
        vb = v_ref[rows(n), :]
        outs = []
        for h in range(2):
            cc = c_ref[h]
            outs.append(_dot(jnp.exp2(e_ref[h] + jnp.concatenate([cc] * (tq // cc.shape[1]), axis=1)), vb))
            c_ref[h] = cc + tot_ref[h][:, :1]
        o_ref[...] += jnp.where(first, outs[0], outs[1])

    o_ref[...] = jnp.zeros_like(o_ref)
    c_ref[...] = jnp.zeros_like(c_ref)
    stage1(0, masked=True)

    @pl.when(ntiles == 1)
    def _():
        stage2()

    @pl.when(ntiles >= 2)
    def _():
        stage2()
        stage1(1)

    def steady(n, carry):
        stage3(n - 2)
        stage2()
        stage1(n)
        return carry

    lax.fori_loop(2, ntiles, steady, 0)

    @pl.when(ntiles >= 2)
    def _():
        stage3(ntiles - 2)
        stage2()

    stage3(ntiles - 1)


def _pattn(bias, q, k, v, dh):
    b, t, sbw = q.shape
    tq = min(ATTN_BLOCK, t)
    blk = pl.BlockSpec((None, tq, 2 * dh), lambda bi, hp, i: (bi, i, hp))
    full = pl.BlockSpec((None, t, 2 * dh), lambda bi, hp, i: (bi, 0, hp))
    return pl.pallas_call(
        functools.partial(_pattn_kernel, dh=dh),
        grid=(b, sbw // (2 * dh), t // tq),
        in_specs=[pl.BlockSpec(memory_space=pltpu.SMEM), blk, full, full],
        out_specs=blk,
        out_shape=jax.ShapeDtypeStruct((b, t, sbw), F32),
        scratch_shapes=[pltpu.VMEM((2, tq, tq), F32), pltpu.VMEM((2, tq, tq), F32),
                        pltpu.VMEM((2, tq, 2 * dh), F32), pltpu.VMEM((2, tq, 2 * dh), F32)],
        compiler_params=_params("arbitrary", "arbitrary", "arbitrary"),
        name="pattn",
    )(bias, q, k, v)


def _ssm_drive(u, wb_ref):
    gc, gp = wb_ref.shape[0], wb_ref.shape[1] // 2
    cols = LANES * gp // gc
    ub = u.astype(BF16)
    pieces = []
    for part in range(2):
        for m in range(gc // LANES):
            rows = slice(m * LANES, (m + 1) * LANES)
            pieces.append(_dot(ub[:, rows], wb_ref[rows, part * gp + m * cols:part * gp + (m + 1) * cols]))
    return pieces


def _ssm_readout(hre, him, u, gs, cre_ref, cim_ref, d_ref, wglu_ref, bglu_ref):
    gp, gc = cre_ref.shape
    cols = LANES * gp // gc
    hre, him = hre.astype(BF16), him.astype(BF16)
    y = []
    for m in range(gc // LANES):
        st, ch = slice(m * cols, (m + 1) * cols), slice(m * LANES, (m + 1) * LANES)
        y.append(_dot(hre[:, st], cre_ref[st, ch]) - _dot(him[:, st], cim_ref[st, ch]))
    y = jnp.concatenate(y, axis=1) + d_ref[...] * u
    y = jax.nn.gelu(y)
    z = _dot(y.astype(BF16), wglu_ref[...]) + bglu_ref[...]
    w = z.shape[-1] // 2
    return z[:, :w] * jax.nn.sigmoid(z[:, w:]) * gs


def _pssm_kernel(u_ref, gs_ref, wb_ref, coef_ref, cre_ref, cim_ref, d_ref, wglu_ref, bglu_ref,
                 o_ref, hre_ref, him_ref, st_ref, carry_ref):
    tt = u_ref.shape[0]
    gp = cre_ref.shape[0]
    j = pl.program_id(1)

    @pl.when(j == 0)
    def _():
        carry_ref[...] = jnp.zeros_like(carry_ref)

    u = u_ref[...]
    pieces = _ssm_drive(u, wb_ref)
    cols = pieces[0].shape[1]
    for n, piece in enumerate(pieces):
        st_ref[:, n * cols:(n + 1) * cols] = piece

    for lc in range(gp // SCAN_LANES):
        lre = pl.ds(lc * SCAN_LANES, SCAN_LANES)
        lim = pl.ds(gp + lc * SCAN_LANES, SCAN_LANES)

        def group(g, carry, lre=lre, lim=lim):
            cr, ci = carry
            rows = pl.ds(pl.multiple_of(g * SUBLANES, SUBLANES), SUBLANES)
            xr = st_ref[rows, lre]
            xi = st_ref[rows, lim]
            for n, d in enumerate((1, 2, 4)):
                ar = coef_ref[2 * n, :, lre]
                ai = coef_ref[2 * n + 1, :, lre]
                sr = pltpu.roll(xr, d, 0)
                si = pltpu.roll(xi, d, 0)
                xr, xi = xr + (ar * sr - ai * si), xi + (ar * si + ai * sr)
            pr = coef_ref[6, :, lre]
            pi = coef_ref[7, :, lre]
            hr = xr + (pr * cr - pi * ci)
            hi = xi + (pr * ci + pi * cr)
            st_ref[rows, lre] = hr
            st_ref[rows, lim] = hi
            return hr[SUBLANES - 1:, :], hi[SUBLANES - 1:, :]

        cr, ci = lax.fori_loop(0, tt // SUBLANES, group, (carry_ref[:, lre], carry_ref[:, lim]))
        carry_ref[:, lre] = cr
        carry_ref[:, lim] = ci

    o_ref[...] = _ssm_readout(st_ref[:, :gp], st_ref[:, gp:], u, gs_ref[...],
                              cre_ref, cim_ref, d_ref, wglu_ref, bglu_ref)
    hre_ref[...] = carry_ref[:, :gp]
    him_ref[...] = carry_ref[:, gp:]


def _const(shape):
    nd = len(shape)
    return pl.BlockSpec(shape, lambda *_: (0,) * nd)


def _pssm(u, gs, wb, coef, cre, cim, dsk, wglu, bglu):
    b, t, ssw = u.shape
    gp = cre.shape[0]
    tt = min(SSM_ROWS, t)
    row = pl.BlockSpec((None, tt, ssw), lambda i, j: (i, j, 0))
    st = pl.BlockSpec((None, 1, gp), lambda i, j: (i, 0, 0))
    return pl.pallas_call(
        _pssm_kernel,
        grid=(b, t // tt),
        in_specs=[row, row, _const(wb.shape), _const(coef.shape), _const(cre.shape), _const(cim.shape),
                  _const(dsk.shape), _const(wglu.shape), _const(bglu.shape)],
        out_specs=[row, st, st],
        out_shape=[jax.ShapeDtypeStruct((b, t, ssw), F32),
                   jax.ShapeDtypeStruct((b, 1, gp), F32),
                   jax.ShapeDtypeStruct((b, 1, gp), F32)],
        scratch_shapes=[pltpu.VMEM((tt, 2 * gp), F32), pltpu.VMEM((1, 2 * gp), F32)],
        compiler_params=_params("arbitrary", "arbitrary"),
        name="pssm",
    )(u, gs, wb, coef, cre, cim, dsk, wglu, bglu)


def _sssm_kernel(u_ref, gs_ref, h0re_ref, h0im_ref, wb_ref, coef_ref, cre_ref, cim_ref, d_ref,
                 wglu_ref, bglu_ref, o_ref, hre_ref, him_ref):
    gp = cre_ref.shape[0]
    t = pl.program_id(0)

    @pl.when(t == 0)
    def _():
        hre_ref[...] = h0re_ref[...]
        him_ref[...] = h0im_ref[...]

    u = u_ref[...]
    bu = jnp.concatenate(_ssm_drive(u, wb_ref), axis=1)
    ar = coef_ref[6, 0:1, :]
    ai = coef_ref[7, 0:1, :]
    hr, hi = hre_ref[...], him_ref[...]
    nr = ar * hr - ai * hi + bu[:, :gp]
    ni = ar * hi + ai * hr + bu[:, gp:]
    hre_ref[...] = nr
    him_ref[...] = ni
    o_ref[...] = _ssm_readout(nr, ni, u, gs_ref[...], cre_ref, cim_ref, d_ref, wglu_ref, bglu_ref)


def _sssm(u2, gs2, h0re, h0im, wb, coef, cre, cim, dsk, wglu, bglu, ts):
    nb = u2.shape[0]
    ssw = u2.shape[1] // ts
    gp = cre.shape[0]
    row = pl.BlockSpec((nb, ssw), lambda t: (0, t))
    return pl.pallas_call(
        _sssm_kernel,
        grid=(ts,),
        in_specs=[row, row, _const(h0re.shape), _const(h0im.shape), _const(wb.shape), _const(coef.shape),
                  _const(cre.shape), _const(cim.shape), _const(dsk.shape), _const(wglu.shape),
                  _const(bglu.shape)],
        out_specs=[row, _const((nb, gp)), _const((nb, gp))],
        out_shape=[jax.ShapeDtypeStruct((nb, ts * ssw), F32),
                   jax.ShapeDtypeStruct((nb, gp), F32),
                   jax.ShapeDtypeStruct((nb, gp), F32)],
        compiler_params=_params("arbitrary"),
        name="sssm",
    )(u2, gs2, h0re, h0im, wb, coef, cre, cim, dsk, wglu, bglu)


def _sattn_kernel(pt_ref, q_ref, kn_ref, vn_ref, bias_ref, *refs, npg):
    kp, vp = refs[:npg], refs[npg:2 * npg]
    o_ref, qbd_ref, pad_ref, acc_ref, c_ref = refs[2 * npg:]
    del pt_ref
    ts, sbw = q_ref.shape
    heads, dh, page = kp[0].shape
    rows = ts * heads
    step = pl.program_id(1)
    nbias = -bias_ref[...]
    incl_mat = (lax.broadcasted_iota(jnp.int32, (page, page), 0)
                >= lax.broadcasted_iota(jnp.int32, (page, page), 1)).astype(F32)
    head_match = (lax.broadcasted_iota(jnp.int32, (heads, sbw), 0)
                  == lax.broadcasted_iota(jnp.int32, (heads, sbw), 1) // dh)

    def suffix(zn):
        lk = jnp.minimum(zn, 0.0) - jnp.log(1.0 + jnp.exp(-jnp.abs(zn)))
        incl = _dot(lk, incl_mat)
        return incl - zn, incl[:, :1]

    @pl.when(step == 0)
    def _():
        q = q_ref[...]
        for t in range(ts):
            qrow = jnp.broadcast_to(-q[t:t + 1, :], (heads, sbw))
            qbd_ref[t * heads:(t + 1) * heads, :] = jnp.where(head_match, qrow, 0.0)
        pad_ref[...] = jnp.zeros_like(pad_ref)
        pad_ref[0:ts, :] = kn_ref[...]
        zt = _dot_nt(qbd_ref[...], pad_ref[...]) + nbias[:rows]
        key = lax.broadcasted_iota(jnp.int32, (rows, page), 1)
        qidx = lax.broadcasted_iota(jnp.int32, (rows, page), 0) // heads
        e, tot = suffix(jnp.where(key < qidx, zt, MASKED))
        pad_ref[0:ts, :] = vn_ref[...]
        acc_ref[...] = _dot(jnp.exp(e), pad_ref[...])
        c_ref[...] = tot

    qbd = qbd_ref[...]
    zn = jnp.concatenate([_dot(qbd, kp[j][...].reshape(sbw, page)) for j in range(npg)], axis=0) + nbias
    e, tot = suffix(zn)
    run = c_ref[...]
    offs = []
    for j in range(npg):
        offs.append(run)
        run = run + tot[j * rows:(j + 1) * rows]
    c_ref[...] = run
    a = jnp.exp(e + jnp.concatenate(offs, axis=0))
    acc = acc_ref[...]
    for j in range(npg):
        acc = acc + _dot_nt(a[j * rows:(j + 1) * rows], vp[j][...].reshape(sbw, page))
    acc_ref[...] = acc

    @pl.when(step == pl.num_programs(1) - 1)
    def _():
        for t in range(ts):
            blk = jnp.where(head_match, acc[t * heads:(t + 1) * heads, :], 0.0)
            o_ref[t:t + 1, :] = jnp.sum(blk, axis=0, keepdims=True)


def _sattn(page_table, q3, kn3, vn3, bias_col, ckt, cvt):
    nb, ts, sbw = q3.shape
    npages = page_table.shape[1]
    heads, dh, page = ckt.shape[2:]
    npg = min(SAMPLE_PAGES, npages)
    rows = ts * heads
    tok = pl.BlockSpec((None, ts, sbw), lambda b, c, pt: (b, 0, 0))

    def page_spec(j):
        return pl.BlockSpec((None, None, heads, dh, page),
                            lambda b, c, pt, j=j: (0, pt[b, npages - 1 - npg * c - j], 0, 0, 0))

    pages = [page_spec(j) for j in range(npg)]
    grid_spec = pltpu.PrefetchScalarGridSpec(
        num_scalar_prefetch=1,
        grid=(nb, npages // npg),
        in_specs=[tok, tok, tok, pl.BlockSpec(bias_col.shape, lambda b, c, pt: (0, 0))] + pages + pages,
        out_specs=tok,
        scratch_shapes=[pltpu.VMEM((rows, sbw), F32), pltpu.VMEM((page, sbw), F32),
                        pltpu.VMEM((rows, sbw), F32), pltpu.VMEM((rows, 1), F32)],
    )
    return pl.pallas_call(
        functools.partial(_sattn_kernel, npg=npg),
        grid_spec=grid_spec,
        out_shape=jax.ShapeDtypeStruct((nb, ts, sbw), F32),
        compiler_params=_params("arbitrary", "arbitrary"),
        name="sattn",
    )(page_table, q3, kn3, vn3, bias_col, *([ckt] * npg), *([cvt] * npg))


def _outproj_kernel(x_ref, gate_ref, att_ref, gsb_ref, ssm_ref, w_ref, fg_ref, o_ref):
    sbw = att_ref.shape[-1]
    mixed = (_dot((att_ref[...] * gsb_ref[...]).astype(BF16), w_ref[:sbw, :])
             + _dot(ssm_ref[...].astype(BF16), w_ref[sbw:, :]))
    y = x_ref[...] + gate_ref[...] * mixed
    ms = jnp.mean(y * y, axis=-1, keepdims=True)
    o_ref[...] = y * lax.rsqrt(ms + NORM_EPS) * fg_ref[...]


def _outproj_prompt(x, m3, att, gsb, ssm, w_out, fg):
    b, t, d = x.shape
    sbw, ssw = att.shape[-1], ssm.shape[-1]
    tm = min(PROMPT_ROWS, t)

    def row(w):
        return pl.BlockSpec((None, tm, w), lambda i, j: (i, j, 0))

    return pl.pallas_call(
        _outproj_kernel,
        grid=(b, t // tm),
        in_specs=[row(d), pl.BlockSpec((None, 1, d), lambda i, j: (i, 0, 2)),
                  row(sbw), row(sbw), row(ssw), _const(w_out.shape), _const(fg.shape)],
        out_specs=row(d),
        out_shape=jax.ShapeDtypeStruct((b, t, d), F32),
        compiler_params=_params("arbitrary", "arbitrary"),
        name="outproj_prompt",
    )(x, m3, att, gsb, ssm, w_out, fg)


def _outproj_sample(x2, m, att2, gsb2, ssm2, w_out, fg, ts):
    nb = x2.shape[0]
    d = x2.shape[1] // ts
    sbw, ssw = att2.shape[1] // ts, ssm2.shape[1] // ts

    def col(w):
        return pl.BlockSpec((nb, w), lambda t: (0, t))

    return pl.pallas_call(
        _outproj_kernel,
        grid=(ts,),
        in_specs=[col(d), pl.BlockSpec((nb, d), lambda t: (0, 2)),
                  col(sbw), col(sbw), col(ssw), _const(w_out.shape), _const(fg.shape)],
        out_specs=col(d),
        out_shape=jax.ShapeDtypeStruct((nb, ts * d), F32),
        compiler_params=_params("arbitrary"),
        name="outproj_sample",
    )(x2, m, att2, gsb2, ssm2, w_out, fg)


def _block_diag(w):
    g, a, b = w.shape
    eye = jnp.eye(g, dtype=w.dtype)
    return (eye[:, None, :, None] * w[:, :, None, :]).reshape(g * a, g * b)


def kernel(x_prompt, x_sample, c_prompt, c_sample, cache_k, cache_v, page_table, state_ssm_re, state_ssm_im, norm_g, w_ada, b_ada, w_in, sb_bias, ssm_a_re, ssm_a_im, ssm_log_dt, ssm_b_re, ssm_b_im, ssm_c_re, ssm_c_im, ssm_d, w_glu, b_glu, w_out, final_norm_g):
    depth = w_in.shape[0]
    assert depth == 1, "single mixer layer"
    b, t, d = x_prompt.shape
    nb, ts, _ = x_sample.shape
    heads = sb_bias.shape[1]
    dh = cache_k.shape[-1]
    sbw = heads * dh
    n_pool, page = cache_k.shape[1], cache_k.shape[2]
    g, p = ssm_a_re.shape[1:]
    ch = ssm_b_re.shape[-1]
    gp, ssw = g * p, g * ch
    qscale = float(dh) ** -0.5

    w_in_b = w_in[0].astype(BF16)
    w_out_b = w_out[0].astype(BF16)
    w_glu_b = w_glu[0].astype(BF16)
    ng = norm_g[0].reshape(1, d)
    fg = final_norm_g.reshape(1, d)
    bglu = b_glu[0].reshape(1, -1)
    dsk = ssm_d[0].reshape(1, ssw)
    are = ssm_a_re[0].reshape(1, gp)
    aim = ssm_a_im[0].reshape(1, gp)
    ldt = jnp.repeat(ssm_log_dt[0], p).reshape(1, gp)
    bre_bd = _block_diag(jnp.swapaxes(ssm_b_re[0], 1, 2))
    bim_bd = _block_diag(jnp.swapaxes(ssm_b_im[0], 1, 2))
    cre_bd = _block_diag(jnp.swapaxes(ssm_c_re[0], 1, 2)).astype(BF16)
    cim_bd = _block_diag(jnp.swapaxes(ssm_c_im[0], 1, 2)).astype(BF16)

    m = _ada(jnp.concatenate([c_prompt, c_sample], axis=0), w_ada[0], b_ada[0].reshape(1, -1))
    m_p = m[:b].reshape(b, 1, 3 * d)
    m_s = m[b:]
    coef, wb = _ssm_param(are, aim, ldt, bre_bd, bim_bd)

    q, k_p, v_p, gsb, u, gssm = _inproj_prompt(x_prompt, ng, m_p, w_in_b, sbw, qscale * LOG2E)
    att = _pattn(sb_bias[0], q, k_p, v_p, dh)
    ssm, hre_p, him_p = _pssm(u, gssm, wb, coef, cre_bd, cim_bd, dsk, w_glu_b, bglu)
    y_p = _outproj_prompt(x_prompt, m_p, att, gsb, ssm, w_out_b, fg)

    x2 = x_sample.reshape(nb, ts * d)
    q2, k2, v2, gsb2, u2, gssm2 = _inproj_sample(x2, ng, m_s, w_in_b, sbw, qscale, ts)
    bias_col = jnp.tile(sb_bias[0], ts * min(SAMPLE_PAGES, page_table.shape[1])).reshape(-1, 1)
    att_s = _sattn(page_table, q2.reshape(nb, ts, sbw), k2.reshape(nb, ts, sbw), v2.reshape(nb, ts, sbw),
                   bias_col, jnp.transpose(cache_k, (0, 1, 3, 4, 2)), jnp.transpose(cache_v, (0, 1, 3, 4, 2)))
    ssm2, hre_s, him_s = _sssm(u2, gssm2, state_ssm_re[0].reshape(nb, gp), state_ssm_im[0].reshape(nb, gp),
                               wb, coef, cre_bd, cim_bd, dsk, w_glu_b, bglu, ts)
    y_s = _outproj_sample(x2, m_s, att_s.reshape(nb, ts * sbw), gsb2, ssm2, w_out_b, fg, ts)

    return (y_p, y_s.reshape(nb, ts, d),
            k_p.reshape(1, b, t, heads, dh), v_p.reshape(1, b, t, heads, dh),
            hre_p.reshape(1, b, g, p), him_p.reshape(1, b, g, p),
            k2.reshape(1, nb, ts, heads, dh), v2.reshape(1, nb, ts, heads, dh),
            hre_s.reshape(1, nb, g, p), him_s.reshape(1, nb, g, p))
```
